```python
import jax, jax.numpy as jnp
from jax import lax
import numpy as np

D_MODEL = 1024
BATCH = 2
SEQ = 8192
DEPTH = 4
DEC_BATCH = 128
DEC_SEQ = 1
PAST_LEN = 2048
PAGE_SIZE = 128

N_MIXERS = 2
N_MLSTM = (DEPTH + 1) // 2
N_FOX = DEPTH // 2
M_HEADS = 4
M_DV = D_MODEL // M_HEADS
M_DK = M_DV // 2
M_QK = M_HEADS * M_DK
M_VW = M_HEADS * M_DV
M_IN = 2 * M_QK + 2 * M_VW + 2 * M_HEADS
M_CHUNK = 64
GATE_CAP = 15.0
F_HEADS = 16
F_HD = D_MODEL // F_HEADS
F_IN = 3 * D_MODEL + F_HEADS
Q_BLOCK = 128
D_FF = 4 * D_MODEL
EPS = 1e-6

kernel_name = "mlstm_fox_hybrid_decode_step"


def rmsnorm(x, g):
    xf = x.astype(jnp.float32)
    y = xf * lax.rsqrt(jnp.mean(xf * xf, axis=-1, keepdims=True) + EPS)
    return (y * g.astype(jnp.float32)).astype(x.dtype)


def softcap(z):
    return GATE_CAP * jnp.tanh(z / GATE_CAP)


def sqrelu_mlp(x, w_up, w_down):
    h = jax.nn.relu(x @ w_up)
    return (h * h) @ w_down


def mlstm_scan(q, k, v, li, lf, C0, n0, m0):
    B, L, H, DK = q.shape
    DV = v.shape[-1]
    T = M_CHUNK if L % M_CHUNK == 0 else L
    NC = L // T

    def to_chunks(a):
        a = a.reshape((B, NC, T) + a.shape[2:])
        return jnp.moveaxis(a, (1, 2), (0, 3))

    xs = (to_chunks(q), to_chunks(k), to_chunks(v), to_chunks(li), to_chunks(lf))
    causal = jnp.tril(jnp.ones((T, T), dtype=bool))

    def step(carry, inp):
        C, n, m = carry
        qb, kb, vb, lib, lfb = inp
        b = jnp.cumsum(lfb, axis=-1)
        dmat = b[..., :, None] - b[..., None, :] + lib[..., None, :]
        dmat = jnp.where(causal, dmat, -jnp.inf)
        inter = b + m[..., None]
        mt = jnp.maximum(jnp.max(dmat, axis=-1), inter)
        w = jnp.exp(dmat - mt[..., None])
        a = jnp.exp(inter - mt)
        s = jnp.einsum('bhtd,bhsd->bhts', qb, kb) * w
        num = jnp.einsum('bhts,bhsv->bhtv', s, vb) + a[..., None] * jnp.einsum('bhvd,bhtd->bhtv', C, qb)
        den = jnp.sum(s, axis=-1) + a * jnp.einsum('bhd,bhtd->bht', n, qb)
        h = num / jnp.maximum(jnp.abs(den), jnp.exp(-mt))[..., None]
        bl = b[..., -1]
        g = bl[..., None] - b + lib
        m_new = jnp.maximum(bl + m, jnp.max(g, axis=-1))
        wg = jnp.exp(g - m_new[..., None])
        ac = jnp.exp(bl + m - m_new)
        C_new = ac[..., None, None] * C + jnp.einsum('bhsv,bhsd->bhvd', vb * wg[..., None], kb)
        n_new = ac[..., None] * n + jnp.einsum('bhs,bhsd->bhd', wg, kb)
        return (C_new, n_new, m_new), h

    (C, n, m), h = lax.scan(step, (C0, n0, m0), xs)
    h = jnp.moveaxis(h, (0, 3), (1, 2)).reshape(B, L, H, DV)
    return h, C, n, m


def mlstm_mixer(xn, w_in, b_i, b_f, g_hn, w_out, C0, n0, m0):
    f32 = jnp.float32
    B, L, _ = xn.shape
    proj = xn @ w_in
    q, k, v, o, ig, fg = jnp.split(
        proj, [M_QK, 2 * M_QK, 2 * M_QK + M_VW, 2 * M_QK + 2 * M_VW, 2 * M_QK + 2 * M_VW + M_HEADS], axis=-1)
    q = q.reshape(B, L, M_HEADS, M_DK).astype(f32)
    k = k.reshape(B, L, M_HEADS, M_DK).astype(f32) * (M_DK ** -0.5)
    v = v.reshape(B, L, M_HEADS, M_DV).astype(f32)
    li = softcap(ig.astype(f32) + b_i.astype(f32))
    lf = jax.nn.log_sigmoid(softcap(fg.astype(f32) + b_f.astype(f32)))
    h, C, n, m = mlstm_scan(q, k, v, li, lf, C0, n0, m0)
    h = h * lax.rsqrt(jnp.mean(h * h, axis=-1, keepdims=True) + EPS) * g_hn.astype(f32)
    gated = jax.nn.sigmoid(o.astype(f32)) * h.reshape(B, L, M_VW)
    return gated.astype(xn.dtype) @ w_out, C, n, m


def fox_project(xn, w_in, b_f):
    B, L, _ = xn.shape
    proj = xn @ w_in
    q, k, v, fg = jnp.split(proj, [D_MODEL, 2 * D_MODEL, 3 * D_MODEL], axis=-1)
    q = q.reshape(B, L, F_HEADS, F_HD)
    k = k.reshape(B, L, F_HEADS, F_HD)
    v = v.reshape(B, L, F_HEADS, F_HD)
    logf = jax.nn.log_sigmoid(fg.astype(jnp.float32) + b_f.astype(jnp.float32))
    return q, k, v, logf


def fox_attend(q, k, v, cq, ck, pq, pk):
    f32 = jnp.float32
    s = jnp.einsum('bqhd,bkhd->bhqk', q.astype(f32), k.astype(f32)) * (F_HD ** -0.5)
    s = s + jnp.transpose(cq, (0, 2, 1))[..., :, None] - jnp.transpose(ck, (0, 2, 1))[..., None, :]
    s = jnp.where(pk[None, :] <= pq[:, None], s, -jnp.inf)
    p = jax.nn.softmax(s, axis=-1)
    return jnp.einsum('bhqk,bkhd->bqhd', p, v.astype(f32))


def fox_prompt_attention(q, k, v, logf):
    B, S, H, HD = q.shape
    c = jnp.cumsum(logf, axis=1)
    kf = k.astype(jnp.float32)
    vf = v.astype(jnp.float32)
    pk = jnp.arange(S)

    def block(i):
        start = i * Q_BLOCK
        qb = lax.dynamic_slice_in_dim(q, start, Q_BLOCK, axis=1)
        cqb = lax.dynamic_slice_in_dim(c, start, Q_BLOCK, axis=1)
        pq = start + jnp.arange(Q_BLOCK)
        return fox_attend(qb, kf, vf, cqb, c, pq, pk)

    out = lax.map(block, jnp.arange(S // Q_BLOCK))
    return jnp.moveaxis(out, 0, 1).reshape(B, S, H * HD)


def fox_sample_attention(q, k, v, logf, k_pool, v_pool, lf_pool, page_table):
    DB, L, H, HD = q.shape
    past = (PAST_LEN // PAGE_SIZE) * PAGE_SIZE
    kp = k_pool[page_table].reshape(DB, past, H, HD).astype(k.dtype)
    vp = v_pool[page_table].reshape(DB, past, H, HD).astype(v.dtype)
    lfp = lf_pool[page_table].reshape(DB, past, H).astype(jnp.float32)
    K = jnp.concatenate([kp, k], axis=1)
    V = jnp.concatenate([vp, v], axis=1)
    c = jnp.cumsum(jnp.concatenate([lfp, logf], axis=1), axis=1)
    pq = past + jnp.arange(L)
    pk = jnp.arange(past + L)
    out = fox_attend(q, K, V, c[:, past:], c, pq, pk)
    return out.reshape(DB, L, H * HD)


def setup_inputs(seed: int = 0) -> dict:
    key = jax.random.key(seed)
    ks = jax.random.split(key, 24)
    f32 = jnp.float32
    n_pages = PAST_LEN // PAGE_SIZE
    n_used = DEC_BATCH * n_pages
    n_pool = n_used + (n_used + 3) // 4
    nrm = jax.random.normal
    uni = jax.random.uniform
    page_table = jax.random.permutation(ks[8], n_pool)[:n_used].reshape(DEC_BATCH, n_pages).astype(jnp.int32)
    return {
        "x_prompt": nrm(ks[0], (BATCH, SEQ, D_MODEL), f32),
        "x_sample": nrm(ks[1], (DEC_BATCH, DEC_SEQ, D_MODEL), f32),
        "state_mlstm_C": 0.5 * nrm(ks[2], (N_MLSTM, DEC_BATCH, M_HEADS, M_DV, M_DK), f32),
        "state_mlstm_n": 0.5 * nrm(ks[3], (N_MLSTM, DEC_BATCH, M_HEADS, M_DK), f32),
        "state_mlstm_m": uni(ks[4], (N_MLSTM, DEC_BATCH, M_HEADS), f32, 0.0, 4.0),
        "cache_fox_k": nrm(ks[5], (N_FOX, n_pool, PAGE_SIZE, F_HEADS, F_HD), f32),
        "cache_fox_v": nrm(ks[6], (N_FOX, n_pool, PAGE_SIZE, F_HEADS, F_HD), f32),
        "cache_fox_logf": jax.nn.log_sigmoid(3.0 + nrm(ks[7], (N_FOX, n_pool, PAGE_SIZE, F_HEADS), f32)),
        "page_table": page_table,
        "norm_mix_g": 1.0 + 0.02 * nrm(ks[9], (DEPTH, D_MODEL), f32),
        "norm_ffn_g": 1.0 + 0.02 * nrm(ks[10], (DEPTH, D_MODEL), f32),
        "norm_final_g": 1.0 + 0.02 * nrm(ks[11], (D_MODEL,), f32),
        "mlstm_w_in": nrm(ks[12], (N_MLSTM, D_MODEL, M_IN), f32) * D_MODEL ** -0.5,
        "mlstm_b_i": 0.1 * nrm(ks[13], (N_MLSTM, M_HEADS), f32),
        "mlstm_b_f": 3.0 + 3.0 * uni(ks[14], (N_MLSTM, M_HEADS), f32),
        "mlstm_g_hn": 1.0 + 0.02 * nrm(ks[15], (N_MLSTM, M_HEADS, M_DV), f32),
        "mlstm_w_out": nrm(ks[16], (N_MLSTM, M_VW, D_MODEL), f32) * M_VW ** -0.5,
        "fox_w_in": nrm(ks[17], (N_FOX, D_MODEL, F_IN), f32) * D_MODEL ** -0.5,
        "fox_b_f": 2.0 + 3.0 * uni(ks[18], (N_FOX, F_HEADS), f32),
        "fox_w_out": nrm(ks[19], (N_FOX, D_MODEL, D_MODEL), f32) * D_MODEL ** -0.5,
        "ffn_w_up": nrm(ks[20], (DEPTH, D_MODEL, D_FF), f32) * D_MODEL ** -0.5,
        "ffn_w_down": nrm(ks[21], (DEPTH, D_FF, D_MODEL), f32) * D_FF ** -0.5,
    }


def reference(x_prompt, x_sample, state_mlstm_C, state_mlstm_n, state_mlstm_m,
              cache_fox_k, cache_fox_v, cache_fox_logf, page_table,
              norm_mix_g, norm_ffn_g, norm_final_g,
              mlstm_w_in, mlstm_b_i, mlstm_b_f, mlstm_g_hn, mlstm_w_out,
              fox_w_in, fox_b_f, fox_w_out, ffn_w_up, ffn_w_down):
    f32 = jnp.float32
    xp, xs = x_prompt, x_sample
    B = xp.shape[0]
    C0p = jnp.zeros((B, M_HEADS, M_DV, M_DK), f32)
    n0p = jnp.zeros((B, M_HEADS, M_DK), f32)
    m0p = jnp.zeros((B, M_HEADS), f32)
    mCp, mnp, mmp, mCs, mns, mms = [], [], [], [], [], []
    fkp, fvp, flp, fks, fvs, fls = [], [], [], [], [], []
    for layer in range(DEPTH):
        j = layer // N_MIXERS
        hp = rmsnorm(xp, norm_mix_g[layer])
        hs = rmsnorm(xs, norm_mix_g[layer])
        if layer % N_MIXERS == 0:
            op, Cp, n_p, m_p = mlstm_mixer(hp, mlstm_w_in[j], mlstm_b_i[j], mlstm_b_f[j], mlstm_g_hn[j],
                                           mlstm_w_out[j], C0p, n0p, m0p)
            os_, Cs, n_s, m_s = mlstm_mixer(hs, mlstm_w_in[j], mlstm_b_i[j], mlstm_b_f[j], mlstm_g_hn[j],
                                            mlstm_w_out[j], state_mlstm_C[j].astype(f32),
                                            state_mlstm_n[j].astype(f32), state_mlstm_m[j].astype(f32))
            mCp.append(Cp); mnp.append(n_p); mmp.append(m_p)
            mCs.append(Cs); mns.append(n_s); mms.append(m_s)
        else:
            q, k, v, lf = fox_project(hp, fox_w_in[j], fox_b_f[j])
            op = fox_prompt_attention(q, k, v, lf).astype(xp.dtype) @ fox_w_out[j]
            fkp.append(k); fvp.append(v); flp.append(lf)
            qs, k_s, v_s, lf_s = fox_project(hs, fox_w_in[j], fox_b_f[j])
            os_ = fox_sample_attention(qs, k_s, v_s, lf_s, cache_fox_k[j], cache_fox_v[j],
                                       cache_fox_logf[j], page_table).astype(xs.dtype) @ fox_w_out[j]
            fks.append(k_s); fvs.append(v_s); fls.append(lf_s)
        xp = xp + op
        xs = xs + os_
        xp = xp + sqrelu_mlp(rmsnorm(xp, norm_ffn_g[layer]), ffn_w_up[layer], ffn_w_down[layer])
        xs = xs + sqrelu_mlp(rmsnorm(xs, norm_ffn_g[layer]), ffn_w_up[layer], ffn_w_down[layer])
    y_prompt = rmsnorm(xp, norm_final_g)
    y_sample = rmsnorm(xs, norm_final_g)
    return (y_prompt, y_sample,
            jnp.stack(mCp), jnp.stack(mnp), jnp.stack(mmp),
            jnp.stack(mCs), jnp.stack(mns), jnp.stack(mms),
            jnp.stack(fkp), jnp.stack(fvp), jnp.stack(flp),
            jnp.stack(fks), jnp.stack(fvs), jnp.stack(fls))
```

```python
import functools

import jax
import jax.numpy as jnp
from jax import lax
from jax.experimental import pallas as pl
from jax.experimental.pallas import tpu as pltpu

F32 = jnp.float32
BF16 = jnp.bfloat16

EPS = 1e-6
GATE_CAP = 15.0
M_HEADS = 4
F_HEADS = 16
PAGE_SIZE = 128
LANES = 128
VMEM_LIMIT = 56 * 1024 * 1024

ROW_TILE = 512
MLSTM_CHUNK = 256
ATTN_TILE = 512
DEC_SAMPLES = 8


def _dot(a, b):
    return jnp.dot(a, b, preferred_element_type=F32)


def _dot_nt(a, b):
    return lax.dot_general(a, b, (((1,), (1,)), ((), ())), preferred_element_type=F32)


def _dot_tn(a, b):
    return lax.dot_general(a, b, (((0,), (0,)), ((), ())), preferred_element_type=F32)


def _split3(x):
    hi = x.astype(BF16)
    r = x - hi.astype(F32)
    mid = r.astype(BF16)
    lo = (r - mid.astype(F32)).astype(BF16)
    return hi, mid, lo


def _dot3(a, b, split_lhs):
    if split_lhs:
        hi, mid, lo = _split3(a)
        return _dot(hi, b) + _dot(mid, b) + _dot(lo, b)
    hi, mid, lo = _split3(b)
    return _dot(a, hi) + _dot(a, mid) + _dot(a, lo)


def _rmsnorm(x, g):
    return x * lax.rsqrt(jnp.mean(x * x, axis=-1, keepdims=True) + EPS) * g


def _softcap(z):
    return GATE_CAP * jnp.tanh(z / GATE_CAP)


def _log_sigmoid(x):
    return jnp.minimum(x, 0.0) - jnp.log1p(jnp.exp(-jnp.abs(x)))


def _params(*semantics):
    return pltpu.CompilerParams(dimension_semantics=semantics, vmem_limit_bytes=VMEM_LIMIT)


def _const_spec(shape):
    return pl.BlockSpec(shape, lambda *_: (0,) * len(shape), pipeline_mode=pl.Buffered(1))


def _mlstm_proj_body(x_ref, g_ref, wq_ref, wk_ref, wv_ref, wo_ref, wg_ref, wgt_ref, brow_ref, bcol_ref,
                     q_ref, k_ref, v_ref, o_ref, gc_ref, gr_ref, *, k_scale):
    xn = _rmsnorm(x_ref[...], g_ref[...]).astype(BF16)
    q_ref[...] = _dot(xn, wq_ref[...]).astype(BF16)
    k_ref[...] = (_dot(xn, wk_ref[...]) * k_scale).astype(BF16)
    v_ref[...] = _dot(xn, wv_ref[...]).astype(BF16)
    o_ref[...] = _dot(xn, wo_ref[...])
    zc = _softcap(_dot(xn, wg_ref[...]) + brow_ref[...])
    lane = lax.broadcasted_iota(jnp.int32, zc.shape, 1)
    gc_ref[...] = jnp.where(lane < M_HEADS, zc, _log_sigmoid(zc))
    zr = _softcap(_dot_nt(wgt_ref[...], xn) + bcol_ref[:, 0:1])
    sub = lax.broadcasted_iota(jnp.int32, zr.shape, 0)
    gr_ref[...] = jnp.where(sub < M_HEADS, zr, _log_sigmoid(zr))


def _mlstm_proj(x, g, w, tm):
    n, d = x.shape
    qk, vw = w["wq"].shape[1], w["wv"].shape[1]
    dk = qk // M_HEADS
    row = lambda width: pl.BlockSpec((tm, width), lambda i: (i, 0))
    return pl.pallas_call(
        functools.partial(_mlstm_proj_body, k_scale=dk ** -0.5),
        grid=(n // tm,),
        in_specs=[row(d), _const_spec((1, d)), _const_spec((d, qk)), _const_spec((d, qk)),
                  _const_spec((d, vw)), _const_spec((d, vw)), _const_spec((d, LANES)),
                  _const_spec((8, d)), _const_spec((1, LANES)), _const_spec((8, LANES))],
        out_specs=[row(qk), row(qk), row(vw), row(vw), row(LANES),
                   pl.BlockSpec((8, tm), lambda i: (0, i))],
        out_shape=[jax.ShapeDtypeStruct((n, qk), BF16), jax.ShapeDtypeStruct((n, qk), BF16),
                   jax.ShapeDtypeStruct((n, vw), BF16), jax.ShapeDtypeStruct((n, vw), F32),
                   jax.ShapeDtypeStruct((n, LANES), F32), jax.ShapeDtypeStruct((8, n), F32)],
        compiler_params=_params("parallel"),
        name="mlstm_proj",
    )(x, g, w["wq"], w["wk"], w["wv"], w["wo"], w["wg"], w["wgt"], w["brow"], w["bcol"])


def _mlstm_weights(w_in, b_i, b_f):
    d = w_in.shape[0]
    h = M_HEADS
    vw = d
    qk = (w_in.shape[1] - 2 * vw - 2 * h) // 2
    wb = w_in.astype(BF16)
    gates = wb[:, 2 * qk + 2 * vw:]
    bias = jnp.concatenate([b_i, b_f]).astype(F32)
    return {
        "wq": wb[:, :qk], "wk": wb[:, qk:2 * qk], "wv": wb[:, 2 * qk:2 * qk + vw],
        "wo": wb[:, 2 * qk + vw:2 * qk + 2 * vw],
        "wg": jnp.pad(gates, ((0, 0), (0, LANES - 2 * h))),
        "wgt": gates.T,
        "brow": jnp.pad(bias, (0, LANES - 2 * h))[None, :],
        "bcol": jnp.broadcast_to(bias[:, None], (2 * h, LANES)),
    }


def _mlstm_scan_body(q_ref, k_ref, v_ref, o_ref, gc_ref, gr_ref, ghn_ref,
                     out_ref, cfin_ref, nfin_ref, mfin_ref, c_s, n_s, m_s, *, t, dk, dv):
    c_idx = pl.program_id(1)

    @pl.when(c_idx == 0)
    def _():
        c_s[...] = jnp.zeros_like(c_s)
        n_s[...] = jnp.zeros_like(n_s)
        m_s[...] = jnp.zeros_like(m_s)

    row = lax.broadcasted_iota(jnp.int32, (t, t), 0)
    col = lax.broadcasted_iota(jnp.int32, (t, t), 1)
    causal = col <= row
    tri_l = causal.astype(BF16)
    tri_u = (row <= col).astype(BF16)
    gc = gc_ref[...]
    gr = gr_ref[...]
    bcol_all = _dot3(tri_l, gc, split_lhs=False)
    brow_all = _dot3(gr, tri_u, split_lhs=True)

    for h in range(M_HEADS):
        b_col = bcol_all[:, M_HEADS + h:M_HEADS + h + 1]
        b_row = brow_all[M_HEADS + h:M_HEADS + h + 1, :]
        u_col = gc[:, h:h + 1] - b_col
        u_row = gr[h:h + 1, :] - b_row
        m_prev = m_s[h:h + 1, 0:1]
        qh = q_ref[:, h * dk:(h + 1) * dk]
        kh = k_ref[:, h * dk:(h + 1) * dk]
        vh = v_ref[:, h * dv:(h + 1) * dv]
        ch = c_s[h]
        nh = n_s[h:h + 1, :]

        dmat = jnp.where(causal, b_col + u_row, -jnp.inf)
        inter = b_col + m_prev
        mt = jnp.maximum(jnp.max(dmat, axis=1, keepdims=True), inter)
        w = jnp.exp(dmat - mt)
        a = jnp.exp(inter - mt)
        s = _dot_nt(qh, kh) * w
        num = _dot(s.astype(BF16), vh) + a * _dot_nt(qh, ch.astype(BF16))
        den = jnp.sum(s, axis=1, keepdims=True) + a * jnp.sum(qh.astype(F32) * nh, axis=1, keepdims=True)
        hh = num / jnp.maximum(jnp.abs(den), jnp.exp(-mt))
        hn = hh * lax.rsqrt(jnp.mean(hh * hh, axis=1, keepdims=True) + EPS) * ghn_ref[h:h + 1, :]
        gated = jax.nn.sigmoid(o_ref[:, h * dv:(h + 1) * dv]) * hn
        out_ref[:, h * dv:(h + 1) * dv] = gated.astype(BF16)

        bl = b_col[t - 1:t, :]
        g_col = bl + u_col
        g_row = bl + u_row
        m_new = jnp.maximum(bl + m_prev, jnp.max(g_row, axis=1, keepdims=True))
        wg_col = jnp.exp(g_col - m_new)
        wg_row = jnp.exp(g_row - m_new)
        ac = jnp.exp(bl + m_prev - m_new)
        vw = (vh.astype(F32) * wg_col).astype(BF16)
        c_s[h] = ac * ch + _dot_tn(vw, kh)
        n_s[h:h + 1, :] = ac * nh + _dot(wg_row.astype(BF16), kh)
        m_s[h:h + 1, :] = jnp.broadcast_to(m_new, (1, LANES))

    @pl.when(c_idx == pl.num_programs(1) - 1)
    def _():
        cfin_ref[0] = c_s[...]
        nfin_ref[0] = n_s[0:M_HEADS, :]
        mfin_ref[0] = m_s[...]


def _mlstm_scan(q, k, v, o, gc, gr, ghn, batch, t):
    n, qk = q.shape
    vw = v.shape[1]
    dk, dv = qk // M_HEADS, vw // M_HEADS
    nc = n // batch // t
    row = lambda width: pl.BlockSpec((t, width), lambda b, c: (b * nc + c, 0))
    return pl.pallas_call(
        functools.partial(_mlstm_scan_body, t=t, dk=dk, dv=dv),
        grid=(batch, nc),
        in_specs=[row(qk), row(qk), row(vw), row(vw), row(LANES),
                  pl.BlockSpec((8, t), lambda b, c: (0, b * nc + c)),
                  _const_spec((M_HEADS, dv))],
        out_specs=[row(vw),
                   pl.BlockSpec((1, M_HEADS, dv, dk), lambda b, c: (b, 0, 0, 0)),
                   pl.BlockSpec((1, M_HEADS, dk), lambda b, c: (b, 0, 0)),
                   pl.BlockSpec((1, 8, LANES), lambda b, c: (b, 0, 0))],
        out_shape=[jax.ShapeDtypeStruct((n, vw), BF16),
                   jax.ShapeDtypeStruct((batch, M_HEADS, dv, dk), F32),
                   jax.ShapeDtypeStruct((batch, M_HEADS, dk), F32),
                   jax.ShapeDtypeStruct((batch, 8, LANES), F32)],
        scratch_shapes=[pltpu.VMEM((M_HEADS, dv, dk), F32), pltpu.VMEM((8, dk), F32),
                        pltpu.VMEM((8, LANES), F32)],
        compiler_params=_params("arbitrary", "arbitrary"),
        name="mlstm_scan",
    )(q, k, v, o, gc, gr, ghn)


def _mlstm_step_body(q_ref, k_ref, v_ref, o_ref, gc_ref, ghn_ref, c_ref, n_ref, m_ref,
                     out_ref, cn_ref, nn_ref, mn_ref, *, sb, dk, dv):
    gates = gc_ref[...]
    m_all = m_ref[...]
    q_all = q_ref[...].astype(F32)
    k_all = k_ref[...].astype(F32)
    v_all = v_ref[...].astype(F32)
    sig_o = jax.nn.sigmoid(o_ref[...])
    sub = lax.broadcasted_iota(jnp.int32, (8, dv), 0)
    rows = []
    for s in range(sb):
        outs = []
        for h in range(M_HEADS):
            li = gates[s:s + 1, h:h + 1]
            lf = gates[s:s + 1, M_HEADS + h:M_HEADS + h + 1]
            m_prev = m_all[s:s + 1, h:h + 1]
            qf = q_all[s:s + 1, h * dk:(h + 1) * dk]
            kf = k_all[s:s + 1, h * dk:(h + 1) * dk]
            vf = v_all[s:s + 1, h * dv:(h + 1) * dv]
            ch = c_ref[s, h]
            nh = n_ref[s, h:h + 1, :]

            inter = lf + m_prev
            mt = jnp.maximum(li, inter)
            w = jnp.exp(li - mt)
            a = jnp.exp(inter - mt)
            sc = jnp.sum(qf * kf, axis=1, keepdims=True) * w
            cq = _dot_nt(jnp.broadcast_to(qf, (8, dk)).astype(BF16), ch.astype(BF16))[0:1]
            num = sc * vf + a * cq
            den = sc + a * jnp.sum(nh * qf, axis=1, keepdims=True)
            hh = num / jnp.maximum(jnp.abs(den), jnp.exp(-mt))
            hn = hh * lax.rsqrt(jnp.mean(hh * hh, axis=1, keepdims=True) + EPS) * ghn_ref[h:h + 1, :]
            outs.append(sig_o[s:s + 1, h * dv:(h + 1) * dv] * hn)

            vw8 = jnp.where(sub == 0, jnp.broadcast_to(vf * w, (8, dv)), 0.0).astype(BF16)
            k8 = jnp.broadcast_to(kf, (8, dk)).astype(BF16)
            cn_ref[s, h] = a * ch + _dot_tn(vw8, k8)
            nn_ref[s, h:h + 1, :] = a * nh + w * kf
            mn_ref[s:s + 1, h:h + 1] = mt
        rows.append(jnp.concatenate(outs, axis=1))
    out_ref[...] = jnp.concatenate(rows, axis=0).astype(BF16)


def _mlstm_step(q, k, v, o, gc, ghn, c0, n0, m0, sb):
    n, qk = q.shape
    vw = v.shape[1]
    dk, dv = qk // M_HEADS, vw // M_HEADS
    row = lambda width: pl.BlockSpec((sb, width), lambda i: (i, 0))
    cspec = pl.BlockSpec((sb, M_HEADS, dv, dk), lambda i: (i, 0, 0, 0))
    nspec = pl.BlockSpec((sb, M_HEADS, dk), lambda i: (i, 0, 0))
    return pl.pallas_call(
        functools.partial(_mlstm_step_body, sb=sb, dk=dk, dv=dv),
        grid=(n // sb,),
        in_specs=[row(qk), row(qk), row(vw), row(vw), row(LANES), _const_spec((M_HEADS, dv)),
                  cspec, nspec, row(M_HEADS)],
        out_specs=[row(vw), cspec, nspec, row(M_HEADS)],
        out_shape=[jax.ShapeDtypeStruct((n, vw), BF16),
                   jax.ShapeDtypeStruct(c0.shape, F32), jax.ShapeDtypeStruct(n0.shape, F32),
                   jax.ShapeDtypeStruct(m0.shape, F32)],
        compiler_params=_params("parallel"),
        name="mlstm_step",
    )(q, k, v, o, gc, ghn, c0, n0, m0)


def _mix_ffn_body(x_ref, a_ref, wout_ref, g_ref, wup_ref, wdn_ref, gfin_ref, o_ref, *, n_chunks, final):
    x1 = x_ref[...] + _dot(a_ref[...], wout_ref[...])
    xn = _rmsnorm(x1, g_ref[...]).astype(BF16)
    fc = wup_ref.shape[1] // n_chunks
    acc = x1
    for c in range(n_chunks):
        hid = jnp.maximum(_dot(xn, wup_ref[:, c * fc:(c + 1) * fc]), 0.0)
        acc = acc + _dot((hid * hid).astype(BF16), wdn_ref[c * fc:(c + 1) * fc, :])
    o_ref[...] = _rmsnorm(acc, gfin_ref[...]) if final else acc


def _mix_ffn(x, a, w_out, g, w_up, w_down, g_final, tm, final):
    n, d = x.shape
    dff = w_up.shape[1]
    row = lambda width: pl.BlockSpec((tm, width), lambda i: (i, 0))
    return pl.pallas_call(
        functools.partial(_mix_ffn_body, n_chunks=4, final=final),
        grid=(n // tm,),
        in_specs=[row(d), row(d), _const_spec((d, d)), _const_spec((1, d)),
                  _const_spec((d, dff)), _const_spec((dff, d)), _const_spec((1, d))],
        out_specs=row(d),
        out_shape=jax.ShapeDtypeStruct((n, d), F32),
        compiler_params=_params("parallel"),
        name="mix_ffn",
    )(x, a, w_out, g, w_up, w_down, g_final)


def _fox_proj_body(x_ref, g_ref, wq_ref, wk_ref, wv_ref, wf_ref, bf_ref, et_ref,
                   q_ref, k_ref, v_ref, k32_ref, v32_ref, lf_ref, cexp_ref, crow_ref, carry_s,
                   *, q_scale, cumulative):
    xn = _rmsnorm(x_ref[...], g_ref[...]).astype(BF16)
    q_ref[...] = (_dot(xn, wq_ref[...]) * q_scale).astype(BF16)
    k = _dot(xn, wk_ref[...])
    k32_ref[...] = k
    k_ref[...] = k.astype(BF16)
    v = _dot(xn, wv_ref[...])
    v32_ref[...] = v
    v_ref[...] = v.astype(BF16)
    z = _dot(xn, wf_ref[...]) + bf_ref[...]
    lane = lax.broadcasted_iota(jnp.int32, z.shape, 1)
    lf = jnp.where(lane < F_HEADS, _log_sigmoid(z), 0.0)
    lf_ref[...] = lf[:, :F_HEADS]
    if not cumulative:
        return

    @pl.when(pl.program_id(1) == 0)
    def _():
        carry_s[...] = jnp.zeros_like(carry_s)

    tm = z.shape[0]
    row = lax.broadcasted_iota(jnp.int32, (tm, tm), 0)
    col = lax.broadcasted_iota(jnp.int32, (tm, tm), 1)
    tri = (col <= row).astype(BF16)
    c = _dot3(tri, lf, split_lhs=False) + carry_s[...]
    carry_s[...] = c[tm - 1:tm, :]
    cexp_ref[...] = _dot3(c, et_ref[...], split_lhs=True)
    sub = lax.broadcasted_iota(jnp.int32, (F_HEADS, LANES), 0)
    lan = lax.broadcasted_iota(jnp.int32, (F_HEADS, LANES), 1)
    eye = (sub == lan).astype(BF16)
    hi, mid, lo = _split3(c)
    crow_ref[0] = _dot_nt(eye, hi) + _dot_nt(eye, mid) + _dot_nt(eye, lo)


def _fox_proj(x, g, w, batch, tm, cumulative):
    n, d = x.shape
    hd = d // F_HEADS
    ns = n // batch // tm
    row = lambda width: pl.BlockSpec((tm, width), lambda b, i: (b * ns + i, 0))
    seq = n // batch
    out_specs = [row(d), row(d), row(d), row(d), row(d), row(F_HEADS)]
    out_shape = [jax.ShapeDtypeStruct((n, d), BF16), jax.ShapeDtypeStruct((n, d), BF16),
                 jax.ShapeDtypeStruct((n, d), BF16), jax.ShapeDtypeStruct((n, d), F32),
                 jax.ShapeDtypeStruct((n, d), F32), jax.ShapeDtypeStruct((n, F_HEADS), F32)]
    if cumulative:
        out_specs += [row(d), pl.BlockSpec((1, F_HEADS, tm), lambda b, i: (b, 0, i))]
        out_shape += [jax.ShapeDtypeStruct((n, d), F32), jax.ShapeDtypeStruct((batch, F_HEADS, seq), F32)]
        body = functools.partial(_fox_proj_body, q_scale=hd ** -0.5, cumulative=True)
    else:
        def body(*refs):
            _fox_proj_body(*refs[:14], None, None, None, q_scale=hd ** -0.5, cumulative=False)
    return pl.pallas_call(
        body,
        grid=(batch, ns),
        in_specs=[row(d), _const_spec((1, d)), _const_spec((d, d)), _const_spec((d, d)),
                  _const_spec((d, d)), _const_spec((d, LANES)), _const_spec((1, LANES)),
                  _const_spec((LANES, d))],
        out_specs=out_specs,
        out_shape=out_shape,
        scratch_shapes=[pltpu.VMEM((1, LANES), F32)] if cumulative else [],
        compiler_params=_params("arbitrary", "arbitrary"),
        name="fox_proj_prompt" if cumulative else "fox_proj_sample",
    )(x, g, w["wq"], w["wk"], w["wv"], w["wf"], w["bf"], w["et"])


def _head_expand(d):
    hd = d // F_HEADS
    return (jnp.arange(LANES)[:, None] == (jnp.arange(d)[None, :] // hd)).astype(BF16)


def _fox_weights(w_in, b_f):
    d = w_in.shape[0]
    wb = w_in.astype(BF16)
    return {
        "wq": wb[:, :d], "wk": wb[:, d:2 * d], "wv": wb[:, 2 * d:3 * d],
        "wf": jnp.pad(wb[:, 3 * d:], ((0, 0), (0, LANES - F_HEADS))),
        "bf": jnp.pad(b_f.astype(F32), (0, LANES - F_HEADS))[None, :],
        "et": _head_expand(d),
    }


def _fox_attn_body(q_ref, k_ref, v_ref, cq_ref, crow_ref, o_ref, *, tq, tk, hd):
    pair = pl.program_id(1)
    qi = pl.program_id(2)
    q = q_ref[0]
    cexp = cq_ref[0]
    lane = lax.broadcasted_iota(jnp.int32, (tq, 2 * hd), 1)
    row = lax.broadcasted_iota(jnp.int32, (tq, tk), 0)
    col = lax.broadcasted_iota(jnp.int32, (tq, tk), 1)
    on_or_below_diag = col <= row

    outs = []
    for h in range(2):
        mine = (lane < hd) if h == 0 else (lane >= hd)
        qm = jnp.where(mine, q, jnp.zeros_like(q))
        cq = cexp[:, h * hd:h * hd + 1]
        head = 2 * pair + h

        def step(j, carry, masked):
            m, l, acc = carry
            start = pl.multiple_of(j * tk, tk)
            kj = k_ref[0, pl.ds(start, tk), :]
            vj = v_ref[0, pl.ds(start, tk), :]
            ck = crow_ref[0, head, pl.ds(j, 1), :]
            s = _dot_nt(qm, kj) + (cq - ck)
            if masked:
                s = jnp.where(on_or_below_diag, s, -jnp.inf)
            m_new = jnp.maximum(m, jnp.max(s, axis=1, keepdims=True))
            p = jnp.exp(s - m_new)
            alpha = jnp.exp(m - m_new)
            l = alpha * l + jnp.sum(p, axis=1, keepdims=True)
            acc = alpha * acc + _dot(p.astype(BF16), vj)
            return m_new, l, acc

        init = (jnp.full((tq, 1), -jnp.inf, F32), jnp.zeros((tq, 1), F32), jnp.zeros((tq, 2 * hd), F32))
        carry = lax.fori_loop(0, qi, functools.partial(step, masked=False), init)
        _, l, acc = step(qi, carry, True)
        outs.append(acc / l)

    o_ref[0] = jnp.where(lane < hd, outs[0], outs[1]).astype(BF16)


def _fox_attn(q, k, v, cexp, crow, batch, tile):
    n, d = q.shape
    seq = n // batch
    hd = d // F_HEADS
    nq = seq // tile
    q3, k3, v3, c3 = (a.reshape(batch, seq, d) for a in (q, k, v, cexp))
    crow4 = crow.reshape(batch, F_HEADS, nq, tile)
    qspec = pl.BlockSpec((1, tile, 2 * hd), lambda b, p, i: (b, i, p))
    kvspec = pl.BlockSpec((1, seq, 2 * hd), lambda b, p, i: (b, 0, p))
    out = pl.pallas_call(
        functools.partial(_fox_attn_body, tq=tile, tk=tile, hd=hd),
        grid=(batch, F_HEADS // 2, nq),
        in_specs=[qspec, kvspec, kvspec, qspec,
                  pl.BlockSpec((1, F_HEADS, nq, tile), lambda b, p, i: (b, 0, 0, 0))],
        out_specs=qspec,
        out_shape=jax.ShapeDtypeStruct((batch, seq, d), BF16),
        compiler_params=_params("parallel", "parallel", "arbitrary"),
        name="fox_attn",
    )(q3, k3, v3, c3, crow4)
    return out.reshape(n, d)


def _fox_decode_body(pt_ref, q_ref, kc_ref, vc_ref, lfc_ref, kp_ref, vp_ref, lfp_ref, o_ref,
                     m_s, l_s, carry_s, acc_s, *, hd):
    del pt_ref
    pg = pl.program_id(1)
    d = q_ref.shape[-1]
    sub = lax.broadcasted_iota(jnp.int32, (F_HEADS, d), 0)
    lan = lax.broadcasted_iota(jnp.int32, (F_HEADS, d), 1)
    head_lanes = (lan // hd) == sub
    et = jnp.where(head_lanes, 1.0, 0.0).astype(BF16)
    qblk = jnp.where(head_lanes, jnp.broadcast_to(q_ref[0].astype(F32), (F_HEADS, d)), 0.0).astype(BF16)

    def expand(x):
        return _dot3(jnp.broadcast_to(x, (8, F_HEADS)), et, split_lhs=True)[0:1]

    @pl.when(pg == 0)
    def _():
        kc8 = jnp.broadcast_to(kc_ref[0].astype(BF16), (8, d))
        m_s[...] = _dot_nt(kc8, qblk)[0:1]
        l_s[...] = jnp.ones_like(l_s)
        carry_s[...] = lfc_ref[0]
        sub8 = lax.broadcasted_iota(jnp.int32, (8, d), 0)
        acc_s[...] = jnp.where(sub8 == 0, jnp.broadcast_to(vc_ref[0], (8, d)), 0.0)

    lf = lfp_ref[0]
    pr = lax.broadcasted_iota(jnp.int32, (PAGE_SIZE, PAGE_SIZE), 0)
    pc = lax.broadcasted_iota(jnp.int32, (PAGE_SIZE, PAGE_SIZE), 1)
    later = (pc > pr).astype(BF16)
    bias = carry_s[...] + _dot3(later, lf, split_lhs=False)
    carry_s[...] = carry_s[...] + jnp.sum(lf, axis=0, keepdims=True)
    s = _dot_nt(kp_ref[0].astype(BF16), qblk) + bias
    m_old = m_s[...]
    m_new = jnp.maximum(m_old, jnp.max(s, axis=0, keepdims=True))
    alpha = jnp.exp(m_old - m_new)
    p = jnp.exp(s - m_new)
    m_s[...] = m_new
    l_s[...] = alpha * l_s[...] + jnp.sum(p, axis=0, keepdims=True)
    pv = _dot(p.astype(BF16), et) * vp_ref[0]
    acc_s[...] = expand(alpha) * acc_s[...] + jnp.sum(pv.reshape(PAGE_SIZE // 8, 8, d), axis=0)

    @pl.when(pg == pl.num_programs(1) - 1)
    def _():
        o_ref[0] = (jnp.sum(acc_s[...], axis=0, keepdims=True) / expand(l_s[...])).astype(BF16)


def _fox_decode(q, k_cur, v_cur, lf_cur, k_pool, v_pool, lf_pool, page_table):
    n, d = q.shape
    hd = d // F_HEADS
    n_pages = page_table.shape[1]
    n_pool = k_pool.shape[0]
    kp = k_pool.reshape(n_pool, PAGE_SIZE, d)
    vp = v_pool.reshape(n_pool, PAGE_SIZE, d)
    cur = lambda width: pl.BlockSpec((1, 1, width), lambda b, p, pt: (b, 0, 0))
    page = lambda width: pl.BlockSpec((1, PAGE_SIZE, width), lambda b, p, pt: (pt[b, n_pages - 1 - p], 0, 0))
    out = pl.pallas_call(
        functools.partial(_fox_decode_body, hd=hd),
        grid_spec=pltpu.PrefetchScalarGridSpec(
            num_scalar_prefetch=1,
            grid=(n, n_pages),
            in_specs=[cur(d), cur(d), cur(d), cur(F_HEADS), page(d), page(d), page(F_HEADS)],
            out_specs=cur(d),
            scratch_shapes=[pltpu.VMEM((1, F_HEADS), F32), pltpu.VMEM((1, F_HEADS), F32),
                            pltpu.VMEM((1, F_HEADS), F32), pltpu.VMEM((8, d), F32)],
        ),
        out_shape=jax.ShapeDtypeStruct((n, 1, d), BF16),
        compiler_params=_params("parallel", "arbitrary"),
        name="fox_decode",
    )(page_table, q.reshape(n, 1, d), k_cur.reshape(n, 1, d), v_cur.reshape(n, 1, d),
      lf_cur.reshape(n, 1, F_HEADS), kp, vp, lf_pool)
    return out.reshape(n, d)


def kernel(x_prompt, x_sample, state_mlstm_C, state_mlstm_n, state_mlstm_m, cache_fox_k, cache_fox_v, cache_fox_logf, page_table, norm_mix_g, norm_ffn_g, norm_final_g, mlstm_w_in, mlstm_b_i, mlstm_b_f, mlstm_g_hn, mlstm_w_out, fox_w_in, fox_b_f, fox_w_out, ffn_w_up, ffn_w_down):
    batch, seq, d = x_prompt.shape
    dec = x_sample.shape[0]
    depth = norm_mix_g.shape[0]
    hd = d // F_HEADS
    tm = min(ROW_TILE, seq)
    chunk = min(MLSTM_CHUNK, seq)
    tile = min(ATTN_TILE, seq)
    xp = x_prompt.reshape(batch * seq, d)
    xs = x_sample.reshape(dec, d)
    g_final = norm_final_g.astype(F32)[None, :]

    mcp, mnp, mmp, mcs, mns, mms = [], [], [], [], [], []
    fkp, fvp, flp, fks, fvs, fls = [], [], [], [], [], []
    for layer in range(depth):
        j = layer // 2
        g_mix = norm_mix_g[layer].astype(F32)[None, :]
        if layer % 2 == 0:
            w = _mlstm_weights(mlstm_w_in[j], mlstm_b_i[j], mlstm_b_f[j])
            ghn = mlstm_g_hn[j].astype(F32)
            q, k, v, o, gc, gr = _mlstm_proj(xp, g_mix, w, tm)
            ap, c_p, n_p, m_p = _mlstm_scan(q, k, v, o, gc, gr, ghn, batch, chunk)
            mcp.append(c_p); mnp.append(n_p); mmp.append(m_p[:, :M_HEADS, 0])
            q, k, v, o, gc, _ = _mlstm_proj(xs, g_mix, w, dec)
            as_, c_s, n_s, m_s = _mlstm_step(q, k, v, o, gc, ghn, state_mlstm_C[j].astype(F32),
                                             state_mlstm_n[j].astype(F32), state_mlstm_m[j].astype(F32),
                                             min(DEC_SAMPLES, dec))
            mcs.append(c_s); mns.append(n_s); mms.append(m_s)
            w_out = mlstm_w_out[j].astype(BF16)
        else:
            w = _fox_weights(fox_w_in[j], fox_b_f[j])
            q, k, v, k32, v32, lf, cexp, crow = _fox_proj(xp, g_mix, w, batch, tm, True)
            ap = _fox_attn(q, k, v, cexp, crow, batch, tile)
            fkp.append(k32.reshape(batch, seq, F_HEADS, hd)); fvp.append(v32.reshape(batch, seq, F_HEADS, hd))
            flp.append(lf.reshape(batch, seq, F_HEADS))
            q, _, _, k32, v32, lf = _fox_proj(xs, g_mix, w, 1, dec, False)
            as_ = _fox_decode(q, k32, v32, lf, cache_fox_k[j], cache_fox_v[j], cache_fox_logf[j], page_table)
            fks.append(k32.reshape(dec, 1, F_HEADS, hd)); fvs.append(v32.reshape(dec, 1, F_HEADS, hd))
            fls.append(lf.reshape(dec, 1, F_HEADS))
            w_out = fox_w_out[j].astype(BF16)
        g_ffn = norm_ffn_g[layer].astype(F32)[None, :]
        w_up = ffn_w_up[layer].astype(BF16)
        w_down = ffn_w_down[layer].astype(BF16)
        final = layer == depth - 1
        xp = _mix_ffn(xp, ap, w_out, g_ffn, w_up, w_down, g_final, tm, final)
        xs = _mix_ffn(xs, as_, w_out, g_ffn, w_up, w_down, g_final, dec, final)

    return (xp.reshape(batch, seq, d), xs.reshape(dec, 1, d),
            jnp.stack(mcp), jnp.stack(mnp), jnp.stack(mmp),
            jnp.stack(mcs), jnp.stack(mns), jnp.stack(mms),
            jnp.stack(fkp), jnp.stack(fvp), jnp.stack(flp),
            jnp.stack(fks), jnp.stack(fvs), jnp.stack(fls))
```

```python
import functools
import math

import jax
import jax.numpy as jnp
from jax import lax
from jax.experimental import pallas as pl
from jax.experimental.pallas import tpu as pltpu

F32 = jnp.float32
BF16 = jnp.bfloat16

EPS = 1e-6
GATE_CAP = 15.0
M_HEADS = 4
F_HEADS = 16
PAGE_SIZE = 128
LOG2E = math.log2(math.e)
LANES = 128
VMEM_LIMIT = 56 * 1024 * 1024

ROW_TILE = 512
MLSTM_CHUNK = 256
DEC_SAMPLES = 8


def _dot(a, b):
    return jnp.dot(a, b, preferred_element_type=F32)


def _dot_nt(a, b):
    return lax.dot_general(a, b, (((1,), (1,)), ((), ())), preferred_element_type=F32)


def _dot_tn(a, b):
    return lax.dot_general(a, b, (((0,), (0,)), ((), ())), preferred_element_type=F32)


def _split3(x):
    hi = x.astype(BF16)
    r = x - hi.astype(F32)
    mid = r.astype(BF16)
    lo = (r - mid.astype(F32)).astype(BF16)
    return hi, mid, lo


def _dot3(a, b, split_lhs):
    if split_lhs:
        hi, mid, lo = _split3(a)
        return _dot(hi, b) + _dot(mid, b) + _dot(lo, b)
    hi, mid, lo = _split3(b)
    return _dot(a, hi) + _dot(a, mid) + _dot(a, lo)


def _rmsnorm(x, g):
    return x * lax.rsqrt(jnp.mean(x * x, axis=-1, keepdims=True) + EPS) * g


def _softcap(z):
    return GATE_CAP * jnp.tanh(z / GATE_CAP)


def _log_sigmoid(x):
    return jnp.minimum(x, 0.0) - jnp.log1p(jnp.exp(-jnp.abs(x)))


def _params(*semantics):
    return pltpu.CompilerParams(dimension_semantics=semantics, vmem_limit_bytes=VMEM_LIMIT)


def _const_spec(shape):
    return pl.BlockSpec(shape, lambda *_: (0,) * len(shape), pipeline_mode=pl.Buffered(1))


def _mlstm_proj_body(x_ref, g_ref, wq_ref, wk_ref, wv_ref, wo_ref, wg_ref, wgt_ref, brow_ref, bcol_ref,
                     q_ref, k_ref, v_ref, o_ref, gc_ref, gr_ref, *, k_scale):
    xn = _rmsnorm(x_ref[...], g_ref[...]).astype(BF16)
    q_ref[...] = _dot(xn, wq_ref[...]).astype(BF16)
    k_ref[...] = (_dot(xn, wk_ref[...]) * k_scale).astype(BF16)
    v_ref[...] = _dot(xn, wv_ref[...]).astype(BF16)
    o_ref[...] = _dot(xn, wo_ref[...])
    zc = _softcap(_dot(xn, wg_ref[...]) + brow_ref[...])
    lane = lax.broadcasted_iota(jnp.int32, zc.shape, 1)
    gc_ref[...] = jnp.where(lane < M_HEADS, zc, _log_sigmoid(zc))
    zr = _softcap(_dot_nt(wgt_ref[...], xn) + bcol_ref[:, 0:1])
    sub = lax.broadcasted_iota(jnp.int32, zr.shape, 0)
    gr_ref[...] = jnp.where(sub < M_HEADS, zr, _log_sigmoid(zr))


def _mlstm_proj(x, g, w, tm):
    n, d = x.shape
    qk, vw = w["wq"].shape[1], w["wv"].shape[1]
    dk = qk // M_HEADS
    row = lambda width: pl.BlockSpec((tm, width), lambda i: (i, 0))
    return pl.pallas_call(
        functools.partial(_mlstm_proj_body, k_scale=dk ** -0.5),
        grid=(n // tm,),
        in_specs=[row(d), _const_spec((1, d)), _const_spec((d, qk)), _const_spec((d, qk)),
                  _const_spec((d, vw)), _const_spec((d, vw)), _const_spec((d, LANES)),
                  _const_spec((8, d)), _const_spec((1, LANES)), _const_spec((8, LANES))],
        out_specs=[row(qk), row(qk), row(vw), row(vw), row(LANES),
                   pl.BlockSpec((8, tm), lambda i: (0, i))],
        out_shape=[jax.ShapeDtypeStruct((n, qk), BF16), jax.ShapeDtypeStruct((n, qk), BF16),
                   jax.ShapeDtypeStruct((n, vw), BF16), jax.ShapeDtypeStruct((n, vw), F32),
                   jax.ShapeDtypeStruct((n, LANES), F32), jax.ShapeDtypeStruct((8, n), F32)],
        compiler_params=_params("parallel"),
        name="mlstm_proj",
    )(x, g, w["wq"], w["wk"], w["wv"], w["wo"], w["wg"], w["wgt"], w["brow"], w["bcol"])


def _mlstm_weights(w_in, b_i, b_f):
    d = w_in.shape[0]
    h = M_HEADS
    vw = d
    qk = (w_in.shape[1] - 2 * vw - 2 * h) // 2
    wb = w_in.astype(BF16)
    gates = wb[:, 2 * qk + 2 * vw:]
    bias = jnp.concatenate([b_i, b_f]).astype(F32)
    return {
        "wq": wb[:, :qk], "wk": wb[:, qk:2 * qk], "wv": wb[:, 2 * qk:2 * qk + vw],
        "wo": wb[:, 2 * qk + vw:2 * qk + 2 * vw],
        "wg": jnp.pad(gates, ((0, 0), (0, LANES - 2 * h))),
        "wgt": gates.T,
        "brow": jnp.pad(bias, (0, LANES - 2 * h))[None, :],
        "bcol": jnp.broadcast_to(bias[:, None], (2 * h, LANES)),
    }


def _mlstm_scan_body(q_ref, k_ref, v_ref, o_ref, gc_ref, gr_ref, ghn_ref,
                     out_ref, cfin_ref, nfin_ref, mfin_ref, c_s, n_s, m_s, *, t, dk, dv):
    c_idx = pl.program_id(1)

    @pl.when(c_idx == 0)
    def _():
        c_s[...] = jnp.zeros_like(c_s)
        n_s[...] = jnp.zeros_like(n_s)
        m_s[...] = jnp.zeros_like(m_s)

    row = lax.broadcasted_iota(jnp.int32, (t, t), 0)
    col = lax.broadcasted_iota(jnp.int32, (t, t), 1)
    causal = col <= row
    tri_l = causal.astype(BF16)
    tri_u = (row <= col).astype(BF16)
    gc = gc_ref[...]
    gr = gr_ref[...]
    bcol_all = _dot3(tri_l, gc, split_lhs=False)
    brow_all = _dot3(gr, tri_u, split_lhs=True)

    for h in range(M_HEADS):
        b_col = bcol_all[:, M_HEADS + h:M_HEADS + h + 1]
        b_row = brow_all[M_HEADS + h:M_HEADS + h + 1, :]
        u_col = gc[:, h:h + 1] - b_col
        u_row = gr[h:h + 1, :] - b_row
        m_prev = m_s[h:h + 1, 0:1]
        qh = q_ref[:, h * dk:(h + 1) * dk]
        kh = k_ref[:, h * dk:(h + 1) * dk]
        vh = v_ref[:, h * dv:(h + 1) * dv]
        ch = c_s[h]
        nh = n_s[h:h + 1, :]

        dmat = jnp.where(causal, b_col + u_row, -jnp.inf)
        inter = b_col + m_prev
        mt = jnp.maximum(jnp.max(dmat, axis=1, keepdims=True), inter)
        w = jnp.exp(dmat - mt)
        a = jnp.exp(inter - mt)
        s = _dot_nt(qh, kh) * w
        num = _dot(s.astype(BF16), vh) + a * _dot_nt(qh, ch.astype(BF16))
        den = jnp.sum(s, axis=1, keepdims=True) + a * jnp.sum(qh.astype(F32) * nh, axis=1, keepdims=True)
        hh = num / jnp.maximum(jnp.abs(den), jnp.exp(-mt))
        hn = hh * lax.rsqrt(jnp.mean(hh * hh, axis=1, keepdims=True) + EPS) * ghn_ref[h:h + 1, :]
        gated = jax.nn.sigmoid(o_ref[:, h * dv:(h + 1) * dv]) * hn
        out_ref[:, h * dv:(h + 1) * dv] = gated.astype(BF16)

        bl = b_col[t - 1:t, :]
        g_col = bl + u_col
        g_row = bl + u_row
        m_new = jnp.maximum(bl + m_prev, jnp.max(g_row, axis=1, keepdims=True))
        wg_col = jnp.exp(g_col - m_new)
        wg_row = jnp.exp(g_row - m_new)
        ac = jnp.exp(bl + m_prev - m_new)
        vw = (vh.astype(F32) * wg_col).astype(BF16)
        c_s[h] = ac * ch + _dot_tn(vw, kh)
        n_s[h:h + 1, :] = ac * nh + _dot(wg_row.astype(BF16), kh)
        m_s[h:h + 1, :] = jnp.broadcast_to(m_new, (1, LANES))

    @pl.when(c_idx == pl.num_programs(1) - 1)
    def _():
        cfin_ref[0] = c_s[...]
        nfin_ref[0] = n_s[0:M_HEADS, :]
        mfin_ref[0] = m_s[...]


def _mlstm_scan(q, k, v, o, gc, gr, ghn, batch, t):
    n, qk = q.shape
    vw = v.shape[1]
    dk, dv = qk // M_HEADS, vw // M_HEADS
    nc = n // batch // t
    row = lambda width: pl.BlockSpec((t, width), lambda b, c: (b * nc + c, 0))
    return pl.pallas_call(
        functools.partial(_mlstm_scan_body, t=t, dk=dk, dv=dv),
        grid=(batch, nc),
        in_specs=[row(qk), row(qk), row(vw), row(vw), row(LANES),
                  pl.BlockSpec((8, t), lambda b, c: (0, b * nc + c)),
                  _const_spec((M_HEADS, dv))],
        out_specs=[row(vw),
                   pl.BlockSpec((1, M_HEADS, dv, dk), lambda b, c: (b, 0, 0, 0)),
                   pl.BlockSpec((1, M_HEADS, dk), lambda b, c: (b, 0, 0)),
                   pl.BlockSpec((1, 8, LANES), lambda b, c: (b, 0, 0))],
        out_shape=[jax.ShapeDtypeStruct((n, vw), BF16),
                   jax.ShapeDtypeStruct((batch, M_HEADS, dv, dk), F32),
                   jax.ShapeDtypeStruct((batch, M_HEADS, dk), F32),
                   jax.ShapeDtypeStruct((batch, 8, LANES), F32)],
        scratch_shapes=[pltpu.VMEM((M_HEADS, dv, dk), F32), pltpu.VMEM((8, dk), F32),
                        pltpu.VMEM((8, LANES), F32)],
        compiler_params=_params("arbitrary", "arbitrary"),
        name="mlstm_scan",
    )(q, k, v, o, gc, gr, ghn)


def _mlstm_step_body(q_ref, k_ref, v_ref, o_ref, gc_ref, ghn_ref, c_ref, n_ref, m_ref,
                     out_ref, cn_ref, nn_ref, mn_ref, *, sb, dk, dv):
    gates = gc_ref[...]
    m_all = m_ref[...]
    q_all = q_ref[...].astype(F32)
    k_all = k_ref[...].astype(F32)
    v_all = v_ref[...].astype(F32)
    sig_o = jax.nn.sigmoid(o_ref[...])
    sub = lax.broadcasted_iota(jnp.int32, (8, dv), 0)
    rows = []
    for s in range(sb):
        outs = []
        for h in range(M_HEADS):
            li = gates[s:s + 1, h:h + 1]
            lf = gates[s:s + 1, M_HEADS + h:M_HEADS + h + 1]
            m_prev = m_all[s:s + 1, h:h + 1]
            qf = q_all[s:s + 1, h * dk:(h + 1) * dk]
            kf = k_all[s:s + 1, h * dk:(h + 1) * dk]
            vf = v_all[s:s + 1, h * dv:(h + 1) * dv]
            ch = c_ref[s, h]
            nh = n_ref[s, h:h + 1, :]

            inter = lf + m_prev
            mt = jnp.maximum(li, inter)
            w = jnp.exp(li - mt)
            a = jnp.exp(inter - mt)
            sc = jnp.sum(qf * kf, axis=1, keepdims=True) * w
            cq = _dot_nt(jnp.broadcast_to(qf, (8, dk)).astype(BF16), ch.astype(BF16))[0:1]
            num = sc * vf + a * cq
            den = sc + a * jnp.sum(nh * qf, axis=1, keepdims=True)
            hh = num / jnp.maximum(jnp.abs(den), jnp.exp(-mt))
            hn = hh * lax.rsqrt(jnp.mean(hh * hh, axis=1, keepdims=True) + EPS) * ghn_ref[h:h + 1, :]
            outs.append(sig_o[s:s + 1, h * dv:(h + 1) * dv] * hn)

            vw8 = jnp.where(sub == 0, jnp.broadcast_to(vf * w, (8, dv)), 0.0).astype(BF16)
            k8 = jnp.broadcast_to(kf, (8, dk)).astype(BF16)
            cn_ref[s, h] = a * ch + _dot_tn(vw8, k8)
            nn_ref[s, h:h + 1, :] = a * nh + w * kf
            mn_ref[s:s + 1, h:h + 1] = mt
        rows.append(jnp.concatenate(outs, axis=1))
    out_ref[...] = jnp.concatenate(rows, axis=0).astype(BF16)


def _mlstm_step(q, k, v, o, gc, ghn, c_all, n_all, m0, layer, sb):
    n, qk = q.shape
    vw = v.shape[1]
    dk, dv = qk // M_HEADS, vw // M_HEADS
    c_flat = c_all.reshape((-1,) + c_all.shape[2:])
    n_flat = n_all.reshape((-1,) + n_all.shape[2:])
    off = layer * (n // sb)
    row = lambda width: pl.BlockSpec((sb, width), lambda i: (i, 0))
    cshape, nshape = (sb, M_HEADS, dv, dk), (sb, M_HEADS, dk)
    return pl.pallas_call(
        functools.partial(_mlstm_step_body, sb=sb, dk=dk, dv=dv),
        grid=(n // sb,),
        in_specs=[row(qk), row(qk), row(vw), row(vw), row(LANES), _const_spec((M_HEADS, dv)),
                  pl.BlockSpec(cshape, lambda i: (off + i, 0, 0, 0)),
                  pl.BlockSpec(nshape, lambda i: (off + i, 0, 0)), row(M_HEADS)],
        out_specs=[row(vw), pl.BlockSpec(cshape, lambda i: (i, 0, 0, 0)),
                   pl.BlockSpec(nshape, lambda i: (i, 0, 0)), row(M_HEADS)],
        out_shape=[jax.ShapeDtypeStruct((n, vw), BF16),
                   jax.ShapeDtypeStruct(c_all.shape[1:], F32), jax.ShapeDtypeStruct(n_all.shape[1:], F32),
                   jax.ShapeDtypeStruct(m0.shape, F32)],
        compiler_params=_params("parallel"),
        name="mlstm_step",
    )(q, k, v, o, gc, ghn, c_flat, n_flat, m0)


def _mix_ffn_body(x_ref, a_ref, wout_ref, g_ref, wup_ref, wdn_ref, gfin_ref, o_ref, *, n_chunks, final):
    x1 = x_ref[...] + _dot(a_ref[...], wout_ref[...])
    xn = _rmsnorm(x1, g_ref[...]).astype(BF16)
    fc = wup_ref.shape[1] // n_chunks
    acc = x1
    for c in range(n_chunks):
        hid = jnp.maximum(_dot(xn, wup_ref[:, c * fc:(c + 1) * fc]), 0.0)
        acc = acc + _dot((hid * hid).astype(BF16), wdn_ref[c * fc:(c + 1) * fc, :])
    o_ref[...] = _rmsnorm(acc, gfin_ref[...]) if final else acc


def _mix_ffn(x, a, w_out, g, w_up, w_down, g_final, tm, final):
    n, d = x.shape
    dff = w_up.shape[1]
    row = lambda width: pl.BlockSpec((tm, width), lambda i: (i, 0))
    return pl.pallas_call(
        functools.partial(_mix_ffn_body, n_chunks=4, final=final),
        grid=(n // tm,),
        in_specs=[row(d), row(d), _const_spec((d, d)), _const_spec((1, d)),
                  _const_spec((d, dff)), _const_spec((dff, d)), _const_spec((1, d))],
        out_specs=row(d),
        out_shape=jax.ShapeDtypeStruct((n, d), F32),
        compiler_params=_params("parallel"),
        name="mix_ffn",
    )(x, a, w_out, g, w_up, w_down, g_final)


def _head_expand(d):
    hd = d // F_HEADS
    return (jnp.arange(LANES)[:, None] == (jnp.arange(d)[None, :] // hd)).astype(BF16)


def _fox_weights(w_in, b_f):
    d = w_in.shape[0]
    wt = w_in.T.astype(BF16)
    wft = wt[3 * d:]
    bias = b_f.astype(F32)
    return {
        "wqt": wt[:d], "wkt": wt[d:2 * d], "wvt": wt[2 * d:3 * d], "wft": wft,
        "wf": jnp.pad(wft.T, ((0, 0), (0, LANES - F_HEADS))),
        "bfrow": jnp.pad(bias, (0, LANES - F_HEADS))[None, :],
        "bfcol": jnp.broadcast_to(bias[:, None], (F_HEADS, LANES)),
        "et": _head_expand(d),
    }


def _fox_proj_body(x_ref, g_ref, wqt_ref, wkt_ref, wvt_ref, wf_ref, wft_ref, bfrow_ref, bfcol_ref, et_ref,
                   qt_ref, k_ref, k32t_ref, vtb_ref, v32t_ref, lft_ref, crow_ref, cexp_ref,
                   carry_row_s, carry_col_s, *, q_scale):
    @pl.when(pl.program_id(1) == 0)
    def _():
        carry_row_s[...] = jnp.zeros_like(carry_row_s)
        carry_col_s[...] = jnp.zeros_like(carry_col_s)

    xn = _rmsnorm(x_ref[...], g_ref[...]).astype(BF16)
    tm = xn.shape[0]
    qt_ref[0] = (_dot_nt(wqt_ref[...], xn) * q_scale).astype(BF16)
    k_ref[...] = _dot_nt(xn, wkt_ref[...]).astype(BF16)
    k32t_ref[0] = _dot_nt(wkt_ref[...], xn)
    vt = _dot_nt(wvt_ref[...], xn)
    v32t_ref[0] = vt
    vtb_ref[0, 0] = vt.astype(BF16)

    zc = _dot(xn, wf_ref[...]) + bfrow_ref[...]
    lane = lax.broadcasted_iota(jnp.int32, zc.shape, 1)
    lfc = jnp.where(lane < F_HEADS, _log_sigmoid(zc), 0.0)
    lfr = _log_sigmoid(_dot_nt(wft_ref[...], xn) + bfcol_ref[:, 0:1])
    lft_ref[0] = lfr
    row = lax.broadcasted_iota(jnp.int32, (tm, tm), 0)
    col = lax.broadcasted_iota(jnp.int32, (tm, tm), 1)
    ccol = _dot3((col <= row).astype(BF16), lfc, split_lhs=False) + carry_row_s[...]
    carry_row_s[...] = ccol[tm - 1:tm, :]
    crow = _dot3(lfr, (row <= col).astype(BF16), split_lhs=True) + carry_col_s[:, 0:1]
    carry_col_s[...] = jnp.broadcast_to(crow[:, tm - 1:tm], carry_col_s.shape)
    crow_ref[0] = crow * LOG2E
    cexp_ref[...] = _dot3(ccol, et_ref[...], split_lhs=True) * LOG2E


def _fox_proj(x, g, w, batch, tm):
    n, d = x.shape
    hd = d // F_HEADS
    seq = n // batch
    ns = seq // tm
    row = lambda width: pl.BlockSpec((tm, width), lambda b, i: (b * ns + i, 0))
    tspec = lambda rows: pl.BlockSpec((1, rows, tm), lambda b, i: (b, 0, i))
    return pl.pallas_call(
        functools.partial(_fox_proj_body, q_scale=hd ** -0.5 * LOG2E),
        grid=(batch, ns),
        in_specs=[row(d), _const_spec((1, d)), _const_spec((d, d)), _const_spec((d, d)), _const_spec((d, d)),
                  _const_spec((d, LANES)), _const_spec((F_HEADS, d)), _const_spec((1, LANES)),
                  _const_spec((F_HEADS, LANES)), _const_spec((LANES, d))],
        out_specs=[tspec(d), row(d), tspec(d),
                   pl.BlockSpec((1, 1, d, tm), lambda b, i: (b, i, 0, 0)), tspec(d),
                   tspec(F_HEADS), tspec(F_HEADS), row(d)],
        out_shape=[jax.ShapeDtypeStruct((batch, d, seq), BF16), jax.ShapeDtypeStruct((n, d), BF16),
                   jax.ShapeDtypeStruct((batch, d, seq), F32),
                   jax.ShapeDtypeStruct((batch, ns, d, tm), BF16), jax.ShapeDtypeStruct((batch, d, seq), F32),
                   jax.ShapeDtypeStruct((batch, F_HEADS, seq), F32), jax.ShapeDtypeStruct((batch, F_HEADS, seq), F32),
                   jax.ShapeDtypeStruct((n, d), F32)],
        scratch_shapes=[pltpu.VMEM((1, LANES), F32), pltpu.VMEM((F_HEADS, LANES), F32)],
        compiler_params=_params("arbitrary", "arbitrary"),
        name="fox_proj",
    )(x, g, w["wqt"], w["wkt"], w["wvt"], w["wf"], w["wft"], w["bfrow"], w["bfcol"], w["et"])


def _fox_attn_body(qt_ref, k_ref, vt_ref, crow_ref, cexp_ref, o_ref, *, tq, tk, hd):
    pair = pl.program_id(1)
    qi = pl.program_id(2)
    sub = lax.broadcasted_iota(jnp.int32, (2 * hd, tq), 0)
    qt = qt_ref[0].astype(F32)
    qtm = [jnp.where(sub < hd, qt, 0.0).astype(BF16), jnp.where(sub >= hd, qt, 0.0).astype(BF16)]
    cq = [crow_ref[0, pl.ds(2 * pair + h, 1), :] for h in range(2)]
    krow = lax.broadcasted_iota(jnp.int32, (tk, tq), 0)
    qcol = lax.broadcasted_iota(jnp.int32, (tk, tq), 1)
    allowed = krow <= qcol

    def step(j, carry, masked):
        start = pl.multiple_of(j * tk, tk)
        kj = k_ref[0, pl.ds(start, tk), :]
        vj = vt_ref[0, j]
        ce = cexp_ref[0, pl.ds(start, tk), :]
        new = []
        for h in range(2):
            m, l, acc = carry[h]
            t = _dot(kj, qtm[h]) - ce[:, h * hd:h * hd + 1]
            if masked:
                t = jnp.where(allowed, t, -jnp.inf)
            m_new = jnp.maximum(m, jnp.max(t, axis=0, keepdims=True) + cq[h])
            p = jnp.exp2(t + (cq[h] - m_new))
            alpha = jnp.exp2(m - m_new)
            l = alpha * l + jnp.sum(p, axis=0, keepdims=True)
            acc = alpha * acc + _dot(vj, p.astype(BF16))
            new.append((m_new, l, acc))
        return tuple(new)

    one = (jnp.full((1, tq), -jnp.inf, F32), jnp.zeros((1, tq), F32), jnp.zeros((2 * hd, tq), F32))
    carry = lax.fori_loop(0, qi, functools.partial(step, masked=False), (one, one))
    (_, l0, acc0), (_, l1, acc1) = step(qi, carry, True)
    ot = jnp.where(sub < hd, acc0 / l0, acc1 / l1)
    o_ref[0] = ot.T.astype(BF16)


def _fox_attn(qt, k, vtb, crow, cexp, tile):
    batch, d, seq = qt.shape
    hd = d // F_HEADS
    nq = seq // tile
    k3 = k.reshape(batch, seq, d)
    c3 = cexp.reshape(batch, seq, d)
    out = pl.pallas_call(
        functools.partial(_fox_attn_body, tq=tile, tk=tile, hd=hd),
        grid=(batch, F_HEADS // 2, nq),
        in_specs=[pl.BlockSpec((1, 2 * hd, tile), lambda b, p, i: (b, p, i)),
                  pl.BlockSpec((1, seq, 2 * hd), lambda b, p, i: (b, 0, p)),
                  pl.BlockSpec((1, nq, 2 * hd, tile), lambda b, p, i: (b, 0, p, 0)),
                  pl.BlockSpec((1, F_HEADS, tile), lambda b, p, i: (b, 0, i)),
                  pl.BlockSpec((1, seq, 2 * hd), lambda b, p, i: (b, 0, p))],
        out_specs=pl.BlockSpec((1, tile, 2 * hd), lambda b, p, i: (b, i, p)),
        out_shape=jax.ShapeDtypeStruct((batch, seq, d), BF16),
        compiler_params=_params("parallel", "parallel", "arbitrary"),
        name="fox_attn",
    )(qt, k3, vtb, crow, c3)
    return out.reshape(batch * seq, d)


def _fox_proj_sample_body(x_ref, g_ref, wqt_ref, wkt_ref, wvt_ref, wft_ref, bfcol_ref,
                          qt_ref, kt_ref, vt_ref, lft_ref, *, q_scale):
    xn = _rmsnorm(x_ref[...], g_ref[...]).astype(BF16)
    qt_ref[...] = _dot_nt(wqt_ref[...], xn) * q_scale
    kt_ref[...] = _dot_nt(wkt_ref[...], xn)
    vt_ref[...] = _dot_nt(wvt_ref[...], xn)
    lft_ref[...] = _log_sigmoid(_dot_nt(wft_ref[...], xn) + bfcol_ref[:, 0:1])


def _fox_proj_sample(x, g, w):
    n, d = x.shape
    hd = d // F_HEADS
    return pl.pallas_call(
        functools.partial(_fox_proj_sample_body, q_scale=hd ** -0.5),
        grid=(1,),
        in_specs=[_const_spec((n, d)), _const_spec((1, d)), _const_spec((d, d)), _const_spec((d, d)),
                  _const_spec((d, d)), _const_spec((F_HEADS, d)), _const_spec((F_HEADS, LANES))],
        out_specs=[pl.BlockSpec((d, n), lambda i: (0, 0))] * 3 + [pl.BlockSpec((F_HEADS, n), lambda i: (0, 0))],
        out_shape=[jax.ShapeDtypeStruct((d, n), F32)] * 3 + [jax.ShapeDtypeStruct((F_HEADS, n), F32)],
        compiler_params=_params("arbitrary"),
        name="fox_proj_sample",
    )(x, g, w["wqt"], w["wkt"], w["wvt"], w["wft"], w["bfcol"])


def _fox_decode_body(pt_ref, qt_ref, kct_ref, vct_ref, lfct_ref, *rest, n_pages, hd):
    del pt_ref
    k_refs = rest[:n_pages]
    v_refs = rest[n_pages:2 * n_pages]
    lf_refs = rest[2 * n_pages:3 * n_pages]
    o_ref, ot_s = rest[3 * n_pages:]
    b = pl.program_id(0)
    d, nb = qt_ref.shape
    heads = d // hd

    @pl.when(b == 0)
    def _():
        ot_s[...] = jnp.zeros_like(ot_s)

    def head_sum(x):
        return jnp.sum(x.reshape(heads, hd, x.shape[1]), axis=1)

    def head_rows(x):
        return jnp.broadcast_to(x[:, None, :], (heads, hd, x.shape[1])).reshape(d, x.shape[1])

    pick = lax.broadcasted_iota(jnp.int32, (d, nb), 1) == b
    column = lambda ref: jnp.sum(jnp.where(pick, ref[...], 0.0), axis=1, keepdims=True)
    qcol, kcol, vcol = column(qt_ref), column(kct_ref), column(vct_ref)
    pick_h = lax.broadcasted_iota(jnp.int32, (heads, nb), 1) == b
    lf_cur = jnp.sum(jnp.where(pick_h, lfct_ref[...], 0.0), axis=1, keepdims=True)

    ri = lax.broadcasted_iota(jnp.int32, (PAGE_SIZE, PAGE_SIZE), 0)
    ci = lax.broadcasted_iota(jnp.int32, (PAGE_SIZE, PAGE_SIZE), 1)
    later = (ri > ci).astype(BF16)

    suffix = lf_cur
    logits = [None] * n_pages
    for pg in reversed(range(n_pages)):
        lf = lf_refs[pg][0]
        bias = suffix + _dot3(lf, later, split_lhs=True)
        suffix = suffix + jnp.sum(lf, axis=1, keepdims=True)
        logits[pg] = head_sum(k_refs[pg][0] * qcol) + bias

    s_cur = head_sum(qcol * kcol)
    m = s_cur
    for pg in range(n_pages):
        m = jnp.maximum(m, jnp.max(logits[pg], axis=1, keepdims=True))
    p_cur = jnp.exp(s_cur - m)
    l = p_cur
    acc = jnp.zeros((d, PAGE_SIZE), F32)
    for pg in range(n_pages):
        p = jnp.exp(logits[pg] - m)
        l = l + jnp.sum(p, axis=1, keepdims=True)
        acc = acc + v_refs[pg][0] * head_rows(p)
    out = (jnp.sum(acc, axis=1, keepdims=True) + vcol * head_rows(p_cur)) / head_rows(l)
    ot_s[...] = jnp.where(pick, out, ot_s[...])

    @pl.when(b == nb - 1)
    def _():
        o_ref[...] = ot_s[...].T.astype(BF16)


def _fox_decode(qt, kct, vct, lfct, k_cache, v_cache, lf_cache, page_table, layer):
    d, n = qt.shape
    hd = d // F_HEADS
    n_pages = page_table.shape[1]
    n_pool = k_cache.shape[1]
    kp = jnp.transpose(k_cache, (0, 1, 3, 4, 2)).reshape(-1, d, PAGE_SIZE)
    vp = jnp.transpose(v_cache, (0, 1, 3, 4, 2)).reshape(-1, d, PAGE_SIZE)
    lfp = jnp.transpose(lf_cache, (0, 1, 3, 2)).reshape(-1, F_HEADS, PAGE_SIZE)
    off = layer * n_pool
    const = lambda rows: pl.BlockSpec((rows, n), lambda b, pt: (0, 0), pipeline_mode=pl.Buffered(1))
    page = lambda rows, pg: pl.BlockSpec((1, rows, PAGE_SIZE), lambda b, pt: (off + pt[b, pg], 0, 0))
    return pl.pallas_call(
        functools.partial(_fox_decode_body, n_pages=n_pages, hd=hd),
        grid_spec=pltpu.PrefetchScalarGridSpec(
            num_scalar_prefetch=1,
            grid=(n,),
            in_specs=[const(d), const(d), const(d), const(F_HEADS)]
                     + [page(d, pg) for pg in range(n_pages)] * 2
                     + [page(F_HEADS, pg) for pg in range(n_pages)],
            out_specs=pl.BlockSpec((n, d), lambda b, pt: (0, 0)),
            scratch_shapes=[pltpu.VMEM((d, n), F32)],
        ),
        out_shape=jax.ShapeDtypeStruct((n, d), BF16),
        compiler_params=_params("arbitrary"),
        name="fox_decode",
    )(page_table, qt, kct, vct, lfct, *([kp] * n_pages), *([vp] * n_pages), *([lfp] * n_pages))


def kernel(x_prompt, x_sample, state_mlstm_C, state_mlstm_n, state_mlstm_m, cache_fox_k, cache_fox_v, cache_fox_logf, page_table, norm_mix_g, norm_ffn_g, norm_final_g, mlstm_w_in, mlstm_b_i, mlstm_b_f, mlstm_g_hn, mlstm_w_out, fox_w_in, fox_b_f, fox_w_out, ffn_w_up, ffn_w_down):
    batch, seq, d = x_prompt.shape
    dec = x_sample.shape[0]
    depth = norm_mix_g.shape[0]
    hd = d // F_HEADS
    tm = min(ROW_TILE, seq)
    chunk = min(MLSTM_CHUNK, seq)
    xp = x_prompt.reshape(batch * seq, d)
    xs = x_sample.reshape(dec, d)
    g_final = norm_final_g.astype(F32)[None, :]
    c_all = state_mlstm_C.astype(F32)
    n_all = state_mlstm_n.astype(F32)

    mcp, mnp, mmp, mcs, mns, mms = [], [], [], [], [], []
    fkp, fvp, flp, fks, fvs, fls = [], [], [], [], [], []
    for layer in range(depth):
        j = layer // 2
        g_mix = norm_mix_g[layer].astype(F32)[None, :]
        if layer % 2 == 0:
            w = _mlstm_weights(mlstm_w_in[j], mlstm_b_i[j], mlstm_b_f[j])
            ghn = mlstm_g_hn[j].astype(F32)
            q, k, v, o, gc, gr = _mlstm_proj(xp, g_mix, w, tm)
            ap, c_p, n_p, m_p = _mlstm_scan(q, k, v, o, gc, gr, ghn, batch, chunk)
            mcp.append(c_p); mnp.append(n_p); mmp.append(m_p[:, :M_HEADS, 0])
            q, k, v, o, gc, _ = _mlstm_proj(xs, g_mix, w, dec)
            as_, c_s, n_s, m_s = _mlstm_step(q, k, v, o, gc, ghn, c_all, n_all, state_mlstm_m[j].astype(F32),
                                             j, min(DEC_SAMPLES, dec))
            mcs.append(c_s); mns.append(n_s); mms.append(m_s)
            w_out = mlstm_w_out[j].astype(BF16)
        else:
            w = _fox_weights(fox_w_in[j], fox_b_f[j])
            qt, k, k32t, vtb, v32t, lft, crow, cexp = _fox_proj(xp, g_mix, w, batch, tm)
            ap = _fox_attn(qt, k, vtb, crow, cexp, tm)
            fkp.append(k32t.reshape(batch, F_HEADS, hd, seq)); fvp.append(v32t.reshape(batch, F_HEADS, hd, seq))
            flp.append(lft)
            qt, kt, vt, lft = _fox_proj_sample(xs, g_mix, w)
            as_ = _fox_decode(qt, kt, vt, lft, cache_fox_k, cache_fox_v, cache_fox_logf, page_table, j)
            fks.append(kt.reshape(1, F_HEADS, hd, dec)); fvs.append(vt.reshape(1, F_HEADS, hd, dec))
            fls.append(lft.reshape(1, F_HEADS, dec))
            w_out = fox_w_out[j].astype(BF16)
        g_ffn = norm_ffn_g[layer].astype(F32)[None, :]
        w_up = ffn_w_up[layer].astype(BF16)
        w_down = ffn_w_down[layer].astype(BF16)
        final = layer == depth - 1
        xp = _mix_ffn(xp, ap, w_out, g_ffn, w_up, w_down, g_final, tm, final)
        xs = _mix_ffn(xs, as_, w_out, g_ffn, w_up, w_down, g_final, dec, final)

    return (xp.reshape(batch, seq, d), xs.reshape(dec, 1, d),
            jnp.stack(mcp), jnp.stack(mnp), jnp.stack(mmp),
            jnp.stack(mcs), jnp.stack(mns), jnp.stack(mms),
            jnp.transpose(jnp.stack(fkp), (0, 1, 4, 2, 3)), jnp.transpose(jnp.stack(fvp), (0, 1, 4, 2, 3)),
            jnp.transpose(jnp.stack(flp), (0, 1, 3, 2)),
            jnp.transpose(jnp.stack(fks), (0, 4, 1, 2, 3)), jnp.transpose(jnp.stack(fvs), (0, 4, 1, 2, 3)),
            jnp.transpose(jnp.stack(fls), (0, 3, 1, 2)))
```

```python
import functools
import math

import jax
import jax.numpy as jnp
from jax import lax
from jax.experimental import pallas as pl
from jax.experimental.pallas import tpu as pltpu

F32 = jnp.float32
BF16 = jnp.bfloat16

EPS = 1e-6
GATE_CAP = 15.0
M_HEADS = 4
F_HEADS = 16
PAGE_SIZE = 128
LOG2E = math.log2(math.e)
LANES = 128
VMEM_LIMIT = 56 * 1024 * 1024

ROW_TILE = 512
MLSTM_CHUNK = 256
DEC_SAMPLES = 8


def _dot(a, b):
    return jnp.dot(a, b, preferred_element_type=F32)


def _dot_nt(a, b):
    return lax.dot_general(a, b, (((1,), (1,)), ((), ())), preferred_element_type=F32)


def _dot_tn(a, b):
    return lax.dot_general(a, b, (((0,), (0,)), ((), ())), preferred_element_type=F32)


def _split3(x):
    hi = x.astype(BF16)
    r = x - hi.astype(F32)
    mid = r.astype(BF16)
    lo = (r - mid.astype(F32)).astype(BF16)
    return hi, mid, lo


def _dot3(a, b, split_lhs):
    if split_lhs:
        hi, mid, lo = _split3(a)
        return _dot(hi, b) + _dot(mid, b) + _dot(lo, b)
    hi, mid, lo = _split3(b)
    return _dot(a, hi) + _dot(a, mid) + _dot(a, lo)


def _rmsnorm(x, g):
    return x * lax.rsqrt(jnp.mean(x * x, axis=-1, keepdims=True) + EPS) * g


def _softcap(z):
    return GATE_CAP * jnp.tanh(z / GATE_CAP)


def _log_sigmoid(x):
    return jnp.minimum(x, 0.0) - jnp.log1p(jnp.exp(-jnp.abs(x)))


def _params(*semantics):
    return pltpu.CompilerParams(dimension_semantics=semantics, vmem_limit_bytes=VMEM_LIMIT)


def _const_spec(shape):
    return pl.BlockSpec(shape, lambda *_: (0,) * len(shape), pipeline_mode=pl.Buffered(1))


def _mlstm_proj_body(x_ref, g_ref, wq_ref, wk_ref, wv_ref, wo_ref, wg_ref, wgt_ref, brow_ref, bcol_ref,
                     q_ref, k_ref, v_ref, o_ref, gc_ref, gr_ref, *, k_scale):
    xn = _rmsnorm(x_ref[...], g_ref[...]).astype(BF16)
    q_ref[...] = _dot(xn, wq_ref[...]).astype(BF16)
    k_ref[...] = (_dot(xn, wk_ref[...]) * k_scale).astype(BF16)
    v_ref[...] = _dot(xn, wv_ref[...]).astype(BF16)
    o_ref[...] = _dot(xn, wo_ref[...])
    zc = _softcap(_dot(xn, wg_ref[...]) + brow_ref[...])
    lane = lax.broadcasted_iota(jnp.int32, zc.shape, 1)
    gc_ref[...] = jnp.where(lane < M_HEADS, zc, _log_sigmoid(zc))
    zr = _softcap(_dot_nt(wgt_ref[...], xn) + bcol_ref[:, 0:1])
    sub = lax.broadcasted_iota(jnp.int32, zr.shape, 0)
    gr_ref[...] = jnp.where(sub < M_HEADS, zr, _log_sigmoid(zr))


def _mlstm_proj(x, g, w, tm):
    n, d = x.shape
    qk, vw = w["wq"].shape[1], w["wv"].shape[1]
    dk = qk // M_HEADS
    row = lambda width: pl.BlockSpec((tm, width), lambda i: (i, 0))
    return pl.pallas_call(
        functools.partial(_mlstm_proj_body, k_scale=dk ** -0.5),
        grid=(n // tm,),
        in_specs=[row(d), _const_spec((1, d)), _const_spec((d, qk)), _const_spec((d, qk)),
                  _const_spec((d, vw)), _const_spec((d, vw)), _const_spec((d, LANES)),
                  _const_spec((8, d)), _const_spec((1, LANES)), _const_spec((8, LANES))],
        out_specs=[row(qk), row(qk), row(vw), row(vw), row(LANES),
                   pl.BlockSpec((8, tm), lambda i: (0, i))],
        out_shape=[jax.ShapeDtypeStruct((n, qk), BF16), jax.ShapeDtypeStruct((n, qk), BF16),
                   jax.ShapeDtypeStruct((n, vw), BF16), jax.ShapeDtypeStruct((n, vw), F32),
                   jax.ShapeDtypeStruct((n, LANES), F32), jax.ShapeDtypeStruct((8, n), F32)],
        compiler_params=_params("parallel"),
        name="mlstm_proj",
    )(x, g, w["wq"], w["wk"], w["wv"], w["wo"], w["wg"], w["wgt"], w["brow"], w["bcol"])


def _mlstm_weights(w_in, b_i, b_f):
    d = w_in.shape[0]
    h = M_HEADS
    vw = d
    qk = (w_in.shape[1] - 2 * vw - 2 * h) // 2
    wb = w_in.astype(BF16)
    gates = wb[:, 2 * qk + 2 * vw:]
    bias = jnp.concatenate([b_i, b_f]).astype(F32)
    return {
        "wq": wb[:, :qk], "wk": wb[:, qk:2 * qk], "wv": wb[:, 2 * qk:2 * qk + vw],
        "wo": wb[:, 2 * qk + vw:2 * qk + 2 * vw],
        "wg": jnp.pad(gates, ((0, 0), (0, LANES - 2 * h))),
        "wgt": gates.T,
        "brow": jnp.pad(bias, (0, LANES - 2 * h))[None, :],
        "bcol": jnp.broadcast_to(bias[:, None], (2 * h, LANES)),
    }


def _mlstm_scan_body(q_ref, k_ref, v_ref, o_ref, gc_ref, gr_ref, ghn_ref,
                     out_ref, cfin_ref, nfin_ref, mfin_ref, c_s, n_s, m_s, *, t, dk, dv):
    c_idx = pl.program_id(1)

    @pl.when(c_idx == 0)
    def _():
        c_s[...] = jnp.zeros_like(c_s)
        n_s[...] = jnp.zeros_like(n_s)
        m_s[...] = jnp.zeros_like(m_s)

    row = lax.broadcasted_iota(jnp.int32, (t, t), 0)
    col = lax.broadcasted_iota(jnp.int32, (t, t), 1)
    causal = col <= row
    tri_l = causal.astype(BF16)
    tri_u = (row <= col).astype(BF16)
    gc = gc_ref[...]
    gr = gr_ref[...]
    bcol_all = _dot3(tri_l, gc, split_lhs=False)
    brow_all = _dot3(gr, tri_u, split_lhs=True)

    for h in range(M_HEADS):
        b_col = bcol_all[:, M_HEADS + h:M_HEADS + h + 1]
        b_row = brow_all[M_HEADS + h:M_HEADS + h + 1, :]
        u_col = gc[:, h:h + 1] - b_col
        u_row = gr[h:h + 1, :] - b_row
        m_prev = m_s[h:h + 1, 0:1]
        qh = q_ref[:, h * dk:(h + 1) * dk]
        kh = k_ref[:, h * dk:(h + 1) * dk]
        vh = v_ref[:, h * dv:(h + 1) * dv]
        ch = c_s[h]
        nh = n_s[h:h + 1, :]

        dmat = jnp.where(causal, b_col + u_row, -jnp.inf)
        inter = b_col + m_prev
        mt = jnp.maximum(jnp.max(dmat, axis=1, keepdims=True), inter)
        w = jnp.exp(dmat - mt)
        a = jnp.exp(inter - mt)
        s = _dot_nt(qh, kh) * w
        num = _dot(s.astype(BF16), vh) + a * _dot_nt(qh, ch.astype(BF16))
        den = jnp.sum(s, axis=1, keepdims=True) + a * jnp.sum(qh.astype(F32) * nh, axis=1, keepdims=True)
        hh = num / jnp.maximum(jnp.abs(den), jnp.exp(-mt))
        hn = hh * lax.rsqrt(jnp.mean(hh * hh, axis=1, keepdims=True) + EPS) * ghn_ref[h:h + 1, :]
        gated = jax.nn.sigmoid(o_ref[:, h * dv:(h + 1) * dv]) * hn
        out_ref[:, h * dv:(h + 1) * dv] = gated.astype(BF16)

        bl = b_col[t - 1:t, :]
        g_col = bl + u_col
        g_row = bl + u_row
        m_new = jnp.maximum(bl + m_prev, jnp.max(g_row, axis=1, keepdims=True))
        wg_col = jnp.exp(g_col - m_new)
        wg_row = jnp.exp(g_row - m_new)
        ac = jnp.exp(bl + m_prev - m_new)
        vw = (vh.astype(F32) * wg_col).astype(BF16)
        c_s[h] = ac * ch + _dot_tn(vw, kh)
        n_s[h:h + 1, :] = ac * nh + _dot(wg_row.astype(BF16), kh)
        m_s[h:h + 1, :] = jnp.broadcast_to(m_new, (1, LANES))

    @pl.when(c_idx == pl.num_programs(1) - 1)
    def _():
        cfin_ref[0] = c_s[...]
        nfin_ref[0] = n_s[0:M_HEADS, :]
        mfin_ref[0] = m_s[...]


def _mlstm_scan(q, k, v, o, gc, gr, ghn, batch, t):
    n, qk = q.shape
    vw = v.shape[1]
    dk, dv = qk // M_HEADS, vw // M_HEADS
    nc = n // batch // t
    row = lambda width: pl.BlockSpec((t, width), lambda b, c: (b * nc + c, 0))
    return pl.pallas_call(
        functools.partial(_mlstm_scan_body, t=t, dk=dk, dv=dv),
        grid=(batch, nc),
        in_specs=[row(qk), row(qk), row(vw), row(vw), row(LANES),
                  pl.BlockSpec((8, t), lambda b, c: (0, b * nc + c)),
                  _const_spec((M_HEADS, dv))],
        out_specs=[row(vw),
                   pl.BlockSpec((1, M_HEADS, dv, dk), lambda b, c: (b, 0, 0, 0)),
                   pl.BlockSpec((1, M_HEADS, dk), lambda b, c: (b, 0, 0)),
                   pl.BlockSpec((1, 8, LANES), lambda b, c: (b, 0, 0))],
        out_shape=[jax.ShapeDtypeStruct((n, vw), BF16),
                   jax.ShapeDtypeStruct((batch, M_HEADS, dv, dk), F32),
                   jax.ShapeDtypeStruct((batch, M_HEADS, dk), F32),
                   jax.ShapeDtypeStruct((batch, 8, LANES), F32)],
        scratch_shapes=[pltpu.VMEM((M_HEADS, dv, dk), F32), pltpu.VMEM((8, dk), F32),
                        pltpu.VMEM((8, LANES), F32)],
        compiler_params=_params("arbitrary", "arbitrary"),
        name="mlstm_scan",
    )(q, k, v, o, gc, gr, ghn)


def _mlstm_step_body(q_ref, k_ref, v_ref, o_ref, gc_ref, ghn_ref, c_ref, n_ref, m_ref,
                     out_ref, cn_ref, nn_ref, mn_ref, *, sb, dk, dv):
    gates = gc_ref[...]
    m_all = m_ref[...]
    q_all = q_ref[...].astype(F32)
    k_all = k_ref[...].astype(F32)
    v_all = v_ref[...].astype(F32)
    sig_o = jax.nn.sigmoid(o_ref[...])
    sub = lax.broadcasted_iota(jnp.int32, (8, dv), 0)
    rows = []
    for s in range(sb):
        outs = []
        for h in range(M_HEADS):
            li = gates[s:s + 1, h:h + 1]
            lf = gates[s:s + 1, M_HEADS + h:M_HEADS + h + 1]
            m_prev = m_all[s:s + 1, h:h + 1]
            qf = q_all[s:s + 1, h * dk:(h + 1) * dk]
            kf = k_all[s:s + 1, h * dk:(h + 1) * dk]
            vf = v_all[s:s + 1, h * dv:(h + 1) * dv]
            ch = c_ref[s, h]
            nh = n_ref[s, h:h + 1, :]

            inter = lf + m_prev
            mt = jnp.maximum(li, inter)
            w = jnp.exp(li - mt)
            a = jnp.exp(inter - mt)
            sc = jnp.sum(qf * kf, axis=1, keepdims=True) * w
            cq = _dot_nt(jnp.broadcast_to(qf, (8, dk)).astype(BF16), ch.astype(BF16))[0:1]
            num = sc * vf + a * cq
            den = sc + a * jnp.sum(nh * qf, axis=1, keepdims=True)
            hh = num / jnp.maximum(jnp.abs(den), jnp.exp(-mt))
            hn = hh * lax.rsqrt(jnp.mean(hh * hh, axis=1, keepdims=True) + EPS) * ghn_ref[h:h + 1, :]
            outs.append(sig_o[s:s + 1, h * dv:(h + 1) * dv] * hn)

            vw8 = jnp.where(sub == 0, jnp.broadcast_to(vf * w, (8, dv)), 0.0).astype(BF16)
            k8 = jnp.broadcast_to(kf, (8, dk)).astype(BF16)
            cn_ref[s, h] = a * ch + _dot_tn(vw8, k8)
            nn_ref[s, h:h + 1, :] = a * nh + w * kf
            mn_ref[s:s + 1, h:h + 1] = mt
        rows.append(jnp.concatenate(outs, axis=1))
    out_ref[...] = jnp.concatenate(rows, axis=0).astype(BF16)


N_MLSTM_STEP_IN = 9


def _mlstm_step(q, k, v, o, gc, ghn, c_all, n_all, m0, layer, sb, c_prev):
    n, qk = q.shape
    vw = v.shape[1]
    dk, dv = qk // M_HEADS, vw // M_HEADS
    flat = lambda a: a.reshape((-1,) + a.shape[2:])
    off = layer * (n // sb)
    row = lambda width: pl.BlockSpec((sb, width), lambda i: (i, 0))
    cspec = pl.BlockSpec((sb, M_HEADS, dv, dk), lambda i: (off + i, 0, 0, 0))
    nshape = (sb, M_HEADS, dk)
    body = functools.partial(_mlstm_step_body, sb=sb, dk=dk, dv=dv)
    operands = [q, k, v, o, gc, ghn, flat(c_all), flat(n_all), m0]
    in_specs = [row(qk), row(qk), row(vw), row(vw), row(LANES), _const_spec((M_HEADS, dv)),
                cspec, pl.BlockSpec(nshape, lambda i: (off + i, 0, 0)), row(M_HEADS)]
    aliases = {}
    if c_prev is not None:
        operands.append(flat(c_prev))
        in_specs.append(pl.BlockSpec(memory_space=pl.ANY))
        aliases = {N_MLSTM_STEP_IN: 1}
        inner = body

        def body(*refs):
            inner(*refs[:N_MLSTM_STEP_IN], *refs[N_MLSTM_STEP_IN + 1:])
    out, c_new, n_new, m_new = pl.pallas_call(
        body,
        grid=(n // sb,),
        in_specs=in_specs,
        out_specs=[row(vw), cspec, pl.BlockSpec(nshape, lambda i: (i, 0, 0)), row(M_HEADS)],
        out_shape=[jax.ShapeDtypeStruct((n, vw), BF16),
                   jax.ShapeDtypeStruct(flat(c_all).shape, F32), jax.ShapeDtypeStruct(n_all.shape[1:], F32),
                   jax.ShapeDtypeStruct(m0.shape, F32)],
        input_output_aliases=aliases,
        compiler_params=_params("parallel"),
        name="mlstm_step",
    )(*operands)
    return out, c_new.reshape(c_all.shape), n_new, m_new


def _mix_ffn_body(x_ref, a_ref, wout_ref, g_ref, wup_ref, wdn_ref, gfin_ref, o_ref, *, n_chunks, final):
    x1 = x_ref[...] + _dot(a_ref[...], wout_ref[...])
    xn = _rmsnorm(x1, g_ref[...]).astype(BF16)
    fc = wup_ref.shape[1] // n_chunks
    acc = x1
    for c in range(n_chunks):
        hid = jnp.maximum(_dot(xn, wup_ref[:, c * fc:(c + 1) * fc]), 0.0)
        acc = acc + _dot((hid * hid).astype(BF16), wdn_ref[c * fc:(c + 1) * fc, :])
    o_ref[...] = _rmsnorm(acc, gfin_ref[...]) if final else acc


def _mix_ffn(x, a, w_out, g, w_up, w_down, g_final, tm, final):
    n, d = x.shape
    dff = w_up.shape[1]
    row = lambda width: pl.BlockSpec((tm, width), lambda i: (i, 0))
    return pl.pallas_call(
        functools.partial(_mix_ffn_body, n_chunks=4, final=final),
        grid=(n // tm,),
        in_specs=[row(d), row(d), _const_spec((d, d)), _const_spec((1, d)),
                  _const_spec((d, dff)), _const_spec((dff, d)), _const_spec((1, d))],
        out_specs=row(d),
        out_shape=jax.ShapeDtypeStruct((n, d), F32),
        compiler_params=_params("parallel"),
        name="mix_ffn",
    )(x, a, w_out, g, w_up, w_down, g_final)


N_SPLIT = 3


def _bias_placement(hd):
    rows = jnp.arange(N_SPLIT * LANES)
    piece, h = rows // LANES, rows % LANES
    target = jnp.where(h < F_HEADS, h * LANES + hd + piece, -1)
    return -(target[:, None] == jnp.arange(F_HEADS * LANES)[None, :]).astype(BF16)


def _fox_weights(w_in, b_f):
    d = w_in.shape[0]
    hd = d // F_HEADS
    wt = w_in.T.astype(BF16)
    wkt, wft = wt[d:2 * d], wt[3 * d:]
    bias = b_f.astype(F32)
    wkt_slots = jnp.pad(wkt.reshape(F_HEADS, hd, d), ((0, 0), (0, LANES - hd), (0, 0))).reshape(F_HEADS * LANES, d)
    return {
        "wqt": wt[:d], "wkt": wkt, "wkt_slots": wkt_slots, "wvt": wt[2 * d:3 * d], "wft": wft,
        "wf": jnp.pad(wft.T, ((0, 0), (0, LANES - F_HEADS))),
        "bfrow": jnp.pad(bias, (0, LANES - F_HEADS))[None, :],
        "bfcol": jnp.broadcast_to(bias[:, None], (F_HEADS, LANES)),
        "place": _bias_placement(hd),
    }


def _fox_proj_body(x_ref, g_ref, wqt_ref, wkt_ref, wks_ref, wvt_ref, wf_ref, wft_ref, bfrow_ref, bfcol_ref, place_ref,
                   qt_ref, ka_ref, k32t_ref, vta_ref, v32t_ref, lft_ref, crow_ref,
                   carry_row_s, carry_col_s, *, q_scale, hd):
    @pl.when(pl.program_id(1) == 0)
    def _():
        carry_row_s[...] = jnp.zeros_like(carry_row_s)
        carry_col_s[...] = jnp.zeros_like(carry_col_s)

    xn = _rmsnorm(x_ref[...], g_ref[...]).astype(BF16)
    tm, d = xn.shape
    qt_ref[0] = (_dot_nt(wqt_ref[...], xn) * q_scale).astype(BF16)
    k32t_ref[0, 0] = _dot_nt(wkt_ref[...], xn)
    vt = _dot_nt(wvt_ref[...], xn)
    v32t_ref[0, 0] = vt
    sub = lax.broadcasted_iota(jnp.int32, (F_HEADS, LANES - hd, tm), 1)
    ones_row = jnp.where(sub == 0, 1.0, 0.0)
    vta = jnp.concatenate([vt.reshape(F_HEADS, hd, tm), ones_row], axis=1)
    vta_ref[0, 0] = vta.reshape(F_HEADS * LANES, tm).astype(BF16)

    zc = _dot(xn, wf_ref[...]) + bfrow_ref[...]
    lane = lax.broadcasted_iota(jnp.int32, zc.shape, 1)
    lfc = jnp.where(lane < F_HEADS, _log_sigmoid(zc), 0.0)
    lfr = _log_sigmoid(_dot_nt(wft_ref[...], xn) + bfcol_ref[:, 0:1])
    lft_ref[0] = lfr
    row = lax.broadcasted_iota(jnp.int32, (tm, tm), 0)
    col = lax.broadcasted_iota(jnp.int32, (tm, tm), 1)
    ccol = _dot3((col <= row).astype(BF16), lfc, split_lhs=False) + carry_row_s[...]
    carry_row_s[...] = ccol[tm - 1:tm, :]
    crow = _dot3(lfr, (row <= col).astype(BF16), split_lhs=True) + carry_col_s[:, 0:1]
    carry_col_s[...] = jnp.broadcast_to(crow[:, tm - 1:tm], carry_col_s.shape)
    crow_ref[0] = crow * LOG2E
    pieces = jnp.concatenate(_split3(ccol * LOG2E), axis=1)
    ka_ref[...] = (_dot_nt(xn, wks_ref[...]) + _dot(pieces, place_ref[...])).astype(BF16)


N_FOX_PROJ_IN = 11


def _fox_proj(x, g, w, batch, tm, slab, n_slabs, kv_prev):
    n, d = x.shape
    hd = d // F_HEADS
    seq = n // batch
    ns = seq // tm
    slots = F_HEADS * LANES
    row = lambda width: pl.BlockSpec((tm, width), lambda b, i: (b * ns + i, 0))
    tspec = lambda rows: pl.BlockSpec((1, rows, tm), lambda b, i: (b, 0, i))
    slab_spec = pl.BlockSpec((1, 1, d, tm), lambda b, i: (slab, b, 0, i))
    slab_shape = jax.ShapeDtypeStruct((n_slabs, batch, d, seq), F32)
    body = functools.partial(_fox_proj_body, q_scale=hd ** -0.5 * LOG2E, hd=hd)
    operands = [x, g, w["wqt"], w["wkt"], w["wkt_slots"], w["wvt"], w["wf"], w["wft"], w["bfrow"], w["bfcol"], w["place"]]
    in_specs = [row(d), _const_spec((1, d)), _const_spec((d, d)), _const_spec((d, d)), _const_spec((slots, d)),
                _const_spec((d, d)), _const_spec((d, LANES)), _const_spec((F_HEADS, d)), _const_spec((1, LANES)),
                _const_spec((F_HEADS, LANES)), _const_spec((N_SPLIT * LANES, slots))]
    aliases = {}
    if kv_prev is not None:
        operands += list(kv_prev)
        in_specs += [pl.BlockSpec(memory_space=pl.ANY)] * 2
        aliases = {N_FOX_PROJ_IN: 2, N_FOX_PROJ_IN + 1: 4}
        inner = body

        def body(*refs):
            inner(*refs[:N_FOX_PROJ_IN], *refs[N_FOX_PROJ_IN + 2:])
    return pl.pallas_call(
        body,
        grid=(batch, ns),
        in_specs=in_specs,
        out_specs=[tspec(d), row(slots), slab_spec,
                   pl.BlockSpec((1, 1, slots, tm), lambda b, i: (b, i, 0, 0)), slab_spec,
                   tspec(F_HEADS), tspec(F_HEADS)],
        out_shape=[jax.ShapeDtypeStruct((batch, d, seq), BF16), jax.ShapeDtypeStruct((n, slots), BF16), slab_shape,
                   jax.ShapeDtypeStruct((batch, ns, slots, tm), BF16), slab_shape,
                   jax.ShapeDtypeStruct((batch, F_HEADS, seq), F32), jax.ShapeDtypeStruct((batch, F_HEADS, seq), F32)],
        scratch_shapes=[pltpu.VMEM((1, LANES), F32), pltpu.VMEM((F_HEADS, LANES), F32)],
        input_output_aliases=aliases,
        compiler_params=_params("arbitrary", "arbitrary"),
        name="fox_proj",
    )(*operands)


def _fox_attn_body(qt_ref, ka_ref, vta_ref, crow_ref, o_ref, s_s, m_s, acc_s, *, tq, tk, hd):
    pair = pl.program_id(1)
    qi = pl.program_id(2)
    sub = lax.broadcasted_iota(jnp.int32, (LANES - hd, tq), 0)
    ones_rows = jnp.where(sub < N_SPLIT, 1.0, 0.0).astype(BF16)
    qa = [jnp.concatenate([qt_ref[0, h * hd:(h + 1) * hd, :], ones_rows], axis=0) for h in range(2)]
    cq = [crow_ref[0, pl.ds(2 * pair + h, 1), :] for h in range(2)]

    def scores(j, slot):
        start = pl.multiple_of(j * tk, tk)
        for h in range(2):
            s_s[slot, h] = _dot(ka_ref[0, pl.ds(start, tk), h * LANES:(h + 1) * LANES], qa[h])

    def update(j, slot, masked):
        for h in range(2):
            t = s_s[slot, h]
            if masked:
                krow = lax.broadcasted_iota(jnp.int32, (tk, tq), 0)
                qcol = lax.broadcasted_iota(jnp.int32, (tk, tq), 1)
                t = jnp.where(krow <= qcol, t, -jnp.inf)
            m = m_s[h]
            m_new = jnp.maximum(m, jnp.max(t, axis=0, keepdims=True) + cq[h])
            p = jnp.exp2(t + (cq[h] - m_new))
            pv = _dot(vta_ref[0, j, h * LANES:(h + 1) * LANES, :], p.astype(BF16))
            acc_s[h] = jnp.exp2(m - m_new) * acc_s[h] + pv
            m_s[h] = m_new

    def finish():
        ot = jnp.concatenate([acc_s[h, :hd, :] / acc_s[h, hd:hd + 1, :] for h in range(2)], axis=0)
        o_ref[0] = ot.T.astype(BF16)

    m_s[...] = jnp.full(m_s.shape, -jnp.inf, F32)
    acc_s[...] = jnp.zeros_like(acc_s)
    scores(0, 0)

    def body(i, _):
        j = 2 * i
        scores(j + 1, 1)
        update(j, 0, False)
        scores(j + 2, 0)
        update(j + 1, 1, False)
        return 0

    lax.fori_loop(0, qi // 2, body, 0)

    @pl.when(qi % 2 == 0)
    def _():
        update(qi, 0, True)
        finish()

    @pl.when(qi % 2 == 1)
    def _():
        scores(qi, 1)
        update(qi - 1, 0, False)
        update(qi, 1, True)
        finish()


def _fox_attn(qt, ka, vta, crow, tile):
    batch, d, seq = qt.shape
    hd = d // F_HEADS
    nq = seq // tile
    slots = F_HEADS * LANES
    out = pl.pallas_call(
        functools.partial(_fox_attn_body, tq=tile, tk=tile, hd=hd),
        grid=(batch, F_HEADS // 2, nq),
        in_specs=[pl.BlockSpec((1, 2 * hd, tile), lambda b, p, i: (b, p, i)),
                  pl.BlockSpec((1, seq, 2 * LANES), lambda b, p, i: (b, 0, p)),
                  pl.BlockSpec((1, nq, 2 * LANES, tile), lambda b, p, i: (b, 0, p, 0)),
                  pl.BlockSpec((1, F_HEADS, tile), lambda b, p, i: (b, 0, i))],
        out_specs=pl.BlockSpec((1, tile, 2 * hd), lambda b, p, i: (b, i, p)),
        out_shape=jax.ShapeDtypeStruct((batch, seq, d), BF16),
        scratch_shapes=[pltpu.VMEM((2, 2, tile, tile), F32), pltpu.VMEM((2, 1, tile), F32),
                        pltpu.VMEM((2, LANES, tile), F32)],
        compiler_params=_params("parallel", "parallel", "arbitrary"),
        name="fox_attn",
    )(qt, ka.reshape(batch, seq, slots), vta, crow)
    return out.reshape(batch * seq, d)


def _fox_proj_sample_body(x_ref, g_ref, wqt_ref, wkt_ref, wvt_ref, wft_ref, bfcol_ref,
                          qt_ref, kt_ref, vt_ref, lft_ref, *, q_scale):
    xn = _rmsnorm(x_ref[...], g_ref[...]).astype(BF16)
    qt_ref[...] = _dot_nt(wqt_ref[...], xn) * q_scale
    kt_ref[...] = _dot_nt(wkt_ref[...], xn)
    vt_ref[...] = _dot_nt(wvt_ref[...], xn)
    lft_ref[...] = _log_sigmoid(_dot_nt(wft_ref[...], xn) + bfcol_ref[:, 0:1])


def _fox_proj_sample(x, g, w):
    n, d = x.shape
    hd = d // F_HEADS
    return pl.pallas_call(
        functools.partial(_fox_proj_sample_body, q_scale=hd ** -0.5),
        grid=(1,),
        in_specs=[_const_spec((n, d)), _const_spec((1, d)), _const_spec((d, d)), _const_spec((d, d)),
                  _const_spec((d, d)), _const_spec((F_HEADS, d)), _const_spec((F_HEADS, LANES))],
        out_specs=[pl.BlockSpec((d, n), lambda i: (0, 0))] * 3 + [pl.BlockSpec((F_HEADS, n), lambda i: (0, 0))],
        out_shape=[jax.ShapeDtypeStruct((d, n), F32)] * 3 + [jax.ShapeDtypeStruct((F_HEADS, n), F32)],
        compiler_params=_params("arbitrary"),
        name="fox_proj_sample",
    )(x, g, w["wqt"], w["wkt"], w["wvt"], w["wft"], w["bfcol"])


def _fox_decode_body(pt_ref, qt_ref, kct_ref, vct_ref, lfct_ref, *rest, n_pages, hd):
    del pt_ref
    k_refs = rest[:n_pages]
    v_refs = rest[n_pages:2 * n_pages]
    lf_refs = rest[2 * n_pages:3 * n_pages]
    o_ref, ot_s = rest[3 * n_pages:]
    b = pl.program_id(0)
    d, nb = qt_ref.shape
    heads = d // hd

    @pl.when(b == 0)
    def _():
        ot_s[...] = jnp.zeros_like(ot_s)

    def head_sum(x):
        return jnp.sum(x.reshape(heads, hd, x.shape[1]), axis=1)

    def head_rows(x):
        return jnp.broadcast_to(x[:, None, :], (heads, hd, x.shape[1])).reshape(d, x.shape[1])

    pick = lax.broadcasted_iota(jnp.int32, (d, nb), 1) == b
    column = lambda ref: jnp.sum(jnp.where(pick, ref[...], 0.0), axis=1, keepdims=True)
    qcol, kcol, vcol = column(qt_ref), column(kct_ref), column(vct_ref)
    pick_h = lax.broadcasted_iota(jnp.int32, (heads, nb), 1) == b
    lf_cur = jnp.sum(jnp.where(pick_h, lfct_ref[...], 0.0), axis=1, keepdims=True)

    ri = lax.broadcasted_iota(jnp.int32, (PAGE_SIZE, PAGE_SIZE), 0)
    ci = lax.broadcasted_iota(jnp.int32, (PAGE_SIZE, PAGE_SIZE), 1)
    later = (ri > ci).astype(BF16)

    suffix = lf_cur
    logits = [None] * n_pages
    for pg in reversed(range(n_pages)):
        lf = lf_refs[pg][0]
        bias = suffix + _dot3(lf, later, split_lhs=True)
        suffix = suffix + jnp.sum(lf, axis=1, keepdims=True)
        logits[pg] = head_sum(k_refs[pg][0] * qcol) + bias

    s_cur = head_sum(qcol * kcol)
    m = s_cur
    for pg in range(n_pages):
        m = jnp.maximum(m, jnp.max(logits[pg], axis=1, keepdims=True))
    p_cur = jnp.exp(s_cur - m)
    l = p_cur
    acc = jnp.zeros((d, PAGE_SIZE), F32)
    for pg in range(n_pages):
        p = jnp.exp(logits[pg] - m)
        l = l + jnp.sum(p, axis=1, keepdims=True)
        acc = acc + v_refs[pg][0] * head_rows(p)
    out = (jnp.sum(acc, axis=1, keepdims=True) + vcol * head_rows(p_cur)) / head_rows(l)
    ot_s[...] = jnp.where(pick, out, ot_s[...])

    @pl.when(b == nb - 1)
    def _():
        o_ref[...] = ot_s[...].T.astype(BF16)


def _fox_decode(qt, kct, vct, lfct, k_cache, v_cache, lf_cache, page_table, layer):
    d, n = qt.shape
    hd = d // F_HEADS
    n_pages = page_table.shape[1]
    n_pool = k_cache.shape[1]
    kp = jnp.transpose(k_cache, (0, 1, 3, 4, 2)).reshape(-1, d, PAGE_SIZE)
    vp = jnp.transpose(v_cache, (0, 1, 3, 4, 2)).reshape(-1, d, PAGE_SIZE)
    lfp = jnp.transpose(lf_cache, (0, 1, 3, 2)).reshape(-1, F_HEADS, PAGE_SIZE)
    off = layer * n_pool
    const = lambda rows: pl.BlockSpec((rows, n), lambda b, pt: (0, 0), pipeline_mode=pl.Buffered(1))
    page = lambda rows, pg: pl.BlockSpec((1, rows, PAGE_SIZE), lambda b, pt: (off + pt[b, pg], 0, 0))
    return pl.pallas_call(
        functools.partial(_fox_decode_body, n_pages=n_pages, hd=hd),
        grid_spec=pltpu.PrefetchScalarGridSpec(
            num_scalar_prefetch=1,
            grid=(n,),
            in_specs=[const(d), const(d), const(d), const(F_HEADS)]
                     + [page(d, pg) for pg in range(n_pages)] * 2
                     + [page(F_HEADS, pg) for pg in range(n_pages)],
            out_specs=pl.BlockSpec((n, d), lambda b, pt: (0, 0)),
            scratch_shapes=[pltpu.VMEM((d, n), F32)],
        ),
        out_shape=jax.ShapeDtypeStruct((n, d), BF16),
        compiler_params=_params("arbitrary"),
        name="fox_decode",
    )(page_table, qt, kct, vct, lfct, *([kp] * n_pages), *([vp] * n_pages), *([lfp] * n_pages))


def kernel(x_prompt, x_sample, state_mlstm_C, state_mlstm_n, state_mlstm_m, cache_fox_k, cache_fox_v, cache_fox_logf, page_table, norm_mix_g, norm_ffn_g, norm_final_g, mlstm_w_in, mlstm_b_i, mlstm_b_f, mlstm_g_hn, mlstm_w_out, fox_w_in, fox_b_f, fox_w_out, ffn_w_up, ffn_w_down):
    batch, seq, d = x_prompt.shape
    dec = x_sample.shape[0]
    depth = norm_mix_g.shape[0]
    hd = d // F_HEADS
    tm = min(ROW_TILE, seq)
    chunk = min(MLSTM_CHUNK, seq)
    xp = x_prompt.reshape(batch * seq, d)
    xs = x_sample.reshape(dec, d)
    g_final = norm_final_g.astype(F32)[None, :]
    c_all = state_mlstm_C.astype(F32)
    n_all = state_mlstm_n.astype(F32)

    mcp, mnp, mmp, mns, mms = [], [], [], [], []
    flp, fks, fvs, fls = [], [], [], []
    c_new = None
    kv32 = None
    for layer in range(depth):
        j = layer // 2
        g_mix = norm_mix_g[layer].astype(F32)[None, :]
        if layer % 2 == 0:
            w = _mlstm_weights(mlstm_w_in[j], mlstm_b_i[j], mlstm_b_f[j])
            ghn = mlstm_g_hn[j].astype(F32)
            q, k, v, o, gc, gr = _mlstm_proj(xp, g_mix, w, tm)
            ap, c_p, n_p, m_p = _mlstm_scan(q, k, v, o, gc, gr, ghn, batch, chunk)
            mcp.append(c_p); mnp.append(n_p); mmp.append(m_p[:, :M_HEADS, 0])
            q, k, v, o, gc, _ = _mlstm_proj(xs, g_mix, w, dec)
            as_, c_new, n_s, m_s = _mlstm_step(q, k, v, o, gc, ghn, c_all, n_all, state_mlstm_m[j].astype(F32),
                                               j, min(DEC_SAMPLES, dec), c_new)
            mns.append(n_s); mms.append(m_s)
            w_out = mlstm_w_out[j].astype(BF16)
        else:
            w = _fox_weights(fox_w_in[j], fox_b_f[j])
            qt, ka, k32t, vta, v32t, lft, crow = _fox_proj(xp, g_mix, w, batch, tm, j, depth // 2, kv32)
            kv32 = (k32t, v32t)
            ap = _fox_attn(qt, ka, vta, crow, tm)
            flp.append(lft)
            qt, kt, vt, lft = _fox_proj_sample(xs, g_mix, w)
            as_ = _fox_decode(qt, kt, vt, lft, cache_fox_k, cache_fox_v, cache_fox_logf, page_table, j)
            fks.append(kt.reshape(1, F_HEADS, hd, dec)); fvs.append(vt.reshape(1, F_HEADS, hd, dec))
            fls.append(lft.reshape(1, F_HEADS, dec))
            w_out = fox_w_out[j].astype(BF16)
        g_ffn = norm_ffn_g[layer].astype(F32)[None, :]
        w_up = ffn_w_up[layer].astype(BF16)
        w_down = ffn_w_down[layer].astype(BF16)
        final = layer == depth - 1
        xp = _mix_ffn(xp, ap, w_out, g_ffn, w_up, w_down, g_final, tm, final)
        xs = _mix_ffn(xs, as_, w_out, g_ffn, w_up, w_down, g_final, dec, final)

    fkp, fvp = (a.reshape(a.shape[0], batch, F_HEADS, hd, seq) for a in kv32)
    return (xp.reshape(batch, seq, d), xs.reshape(dec, 1, d),
            jnp.stack(mcp), jnp.stack(mnp), jnp.stack(mmp),
            c_new, jnp.stack(mns), jnp.stack(mms),
            jnp.transpose(fkp, (0, 1, 4, 2, 3)), jnp.transpose(fvp, (0, 1, 4, 2, 3)),
            jnp.transpose(jnp.stack(flp), (0, 1, 3, 2)),
            jnp.transpose(jnp.stack(fks), (0, 4, 1, 2, 3)), jnp.transpose(jnp.stack(fvs), (0, 4, 1, 2, 3)),
            jnp.transpose(jnp.stack(fls), (0, 3, 1, 2)))
```

```python
import functools
import math

import jax
import jax.numpy as jnp
from jax import lax
from jax.experimental import pallas as pl
from jax.experimental.pallas import tpu as pltpu

F32 = jnp.float32
BF16 = jnp.bfloat16

EPS = 1e-6
GATE_CAP = 15.0
M_HEADS = 4
F_HEADS = 16
PAGE_SIZE = 128
LOG2E = math.log2(math.e)
LANES = 128
VMEM_LIMIT = 56 * 1024 * 1024

ROW_TILE = 512
MLSTM_CHUNK = 256
DEC_SAMPLES = 8


def _dot(a, b):
    return jnp.dot(a, b, preferred_element_type=F32)


def _dot_nt(a, b):
    return lax.dot_general(a, b, (((1,), (1,)), ((), ())), preferred_element_type=F32)


def _dot_tn(a, b):
    return lax.dot_general(a, b, (((0,), (0,)), ((), ())), preferred_element_type=F32)


def _split3(x):
    hi = x.astype(BF16)
    r = x - hi.astype(F32)
    mid = r.astype(BF16)
    lo = (r - mid.astype(F32)).astype(BF16)
    return hi, mid, lo


def _dot3(a, b, split_lhs):
    if split_lhs:
        hi, mid, lo = _split3(a)
        return _dot(hi, b) + _dot(mid, b) + _dot(lo, b)
    hi, mid, lo = _split3(b)
    return _dot(a, hi) + _dot(a, mid) + _dot(a, lo)


def _rmsnorm(x, g):
    return x * lax.rsqrt(jnp.mean(x * x, axis=-1, keepdims=True) + EPS) * g


def _softcap(z):
    return GATE_CAP * jnp.tanh(z / GATE_CAP)


def _log_sigmoid(x):
    return jnp.minimum(x, 0.0) - jnp.log1p(jnp.exp(-jnp.abs(x)))


def _params(*semantics):
    return pltpu.CompilerParams(dimension_semantics=semantics, vmem_limit_bytes=VMEM_LIMIT)


def _const_spec(shape):
    return pl.BlockSpec(shape, lambda *_: (0,) * len(shape), pipeline_mode=pl.Buffered(1))


def _mlstm_proj_body(x_ref, g_ref, wq_ref, wk_ref, wv_ref, wo_ref, wg_ref, wgt_ref, brow_ref, bcol_ref,
                     q_ref, k_ref, v_ref, o_ref, gc_ref, gr_ref, *, k_scale):
    xn = _rmsnorm(x_ref[...], g_ref[...]).astype(BF16)
    q_ref[...] = _dot(xn, wq_ref[...]).astype(BF16)
    k_ref[...] = (_dot(xn, wk_ref[...]) * k_scale).astype(BF16)
    v_ref[...] = _dot(xn, wv_ref[...]).astype(BF16)
    o_ref[...] = _dot(xn, wo_ref[...])
    zc = _softcap(_dot(xn, wg_ref[...]) + brow_ref[...])
    lane = lax.broadcasted_iota(jnp.int32, zc.shape, 1)
    gc_ref[...] = jnp.where(lane < M_HEADS, zc, _log_sigmoid(zc))
    zr = _softcap(_dot_nt(wgt_ref[...], xn) + bcol_ref[:, 0:1])
    sub = lax.broadcasted_iota(jnp.int32, zr.shape, 0)
    gr_ref[...] = jnp.where(sub < M_HEADS, zr, _log_sigmoid(zr))


def _mlstm_proj(x, g, w, tm):
    n, d = x.shape
    qk, vw = w["wq"].shape[1], w["wv"].shape[1]
    dk = qk // M_HEADS
    row = lambda width: pl.BlockSpec((tm, width), lambda i: (i, 0))
    return pl.pallas_call(
        functools.partial(_mlstm_proj_body, k_scale=dk ** -0.5),
        grid=(n // tm,),
        in_specs=[row(d), _const_spec((1, d)), _const_spec((d, qk)), _const_spec((d, qk)),
                  _const_spec((d, vw)), _const_spec((d, vw)), _const_spec((d, LANES)),
                  _const_spec((8, d)), _const_spec((1, LANES)), _const_spec((8, LANES))],
        out_specs=[row(qk), row(qk), row(vw), row(vw), row(LANES),
                   pl.BlockSpec((8, tm), lambda i: (0, i))],
        out_shape=[jax.ShapeDtypeStruct((n, qk), BF16), jax.ShapeDtypeStruct((n, qk), BF16),
                   jax.ShapeDtypeStruct((n, vw), BF16), jax.ShapeDtypeStruct((n, vw), F32),
                   jax.ShapeDtypeStruct((n, LANES), F32), jax.ShapeDtypeStruct((8, n), F32)],
        compiler_params=_params("parallel"),
        name="mlstm_proj",
    )(x, g, w["wq"], w["wk"], w["wv"], w["wo"], w["wg"], w["wgt"], w["brow"], w["bcol"])


def _mlstm_proj_t_body(x_ref, g_ref, wqt_ref, wkt_ref, wvt_ref, wot_ref, wg_ref, wgt_ref, brow_ref, bcol_ref,
                       qt_ref, k_ref, vt_ref, ot_ref, gc_ref, gr_ref, *, k_scale):
    xn = _rmsnorm(x_ref[...], g_ref[...]).astype(BF16)
    qt_ref[...] = _dot_nt(wqt_ref[...], xn).astype(BF16)
    k_ref[...] = (_dot_nt(xn, wkt_ref[...]) * k_scale).astype(BF16)
    vt_ref[...] = _dot_nt(wvt_ref[...], xn).astype(BF16)
    ot_ref[...] = _dot_nt(wot_ref[...], xn)
    zc = _softcap(_dot(xn, wg_ref[...]) + brow_ref[...])
    lane = lax.broadcasted_iota(jnp.int32, zc.shape, 1)
    gc_ref[...] = jnp.where(lane < M_HEADS, zc, _log_sigmoid(zc))
    zr = _softcap(_dot_nt(wgt_ref[...], xn) + bcol_ref[:, 0:1])
    sub = lax.broadcasted_iota(jnp.int32, zr.shape, 0)
    gr_ref[...] = jnp.where(sub < M_HEADS, zr, _log_sigmoid(zr))


def _mlstm_proj_t(x, g, w, tm):
    n, d = x.shape
    qk, vw = w["wqt"].shape[0], w["wvt"].shape[0]
    dk = qk // M_HEADS
    row = lambda width: pl.BlockSpec((tm, width), lambda i: (i, 0))
    tspec = lambda rows: pl.BlockSpec((rows, tm), lambda i: (0, i))
    return pl.pallas_call(
        functools.partial(_mlstm_proj_t_body, k_scale=dk ** -0.5),
        grid=(n // tm,),
        in_specs=[row(d), _const_spec((1, d)), _const_spec((qk, d)), _const_spec((qk, d)),
                  _const_spec((vw, d)), _const_spec((vw, d)), _const_spec((d, LANES)),
                  _const_spec((8, d)), _const_spec((1, LANES)), _const_spec((8, LANES))],
        out_specs=[tspec(qk), row(qk), tspec(vw), tspec(vw), row(LANES), tspec(8)],
        out_shape=[jax.ShapeDtypeStruct((qk, n), BF16), jax.ShapeDtypeStruct((n, qk), BF16),
                   jax.ShapeDtypeStruct((vw, n), BF16), jax.ShapeDtypeStruct((vw, n), F32),
                   jax.ShapeDtypeStruct((n, LANES), F32), jax.ShapeDtypeStruct((8, n), F32)],
        compiler_params=_params("parallel"),
        name="mlstm_proj_t",
    )(x, g, w["wqt"], w["wkt"], w["wvt"], w["wot"], w["wg"], w["wgt"], w["brow"], w["bcol"])


def _mlstm_weights(w_in, b_i, b_f):
    d = w_in.shape[0]
    h = M_HEADS
    vw = d
    qk = (w_in.shape[1] - 2 * vw - 2 * h) // 2
    wb = w_in.astype(BF16)
    wt = w_in.T.astype(BF16)
    gates = wb[:, 2 * qk + 2 * vw:]
    bias = jnp.concatenate([b_i, b_f]).astype(F32)
    return {
        "wq": wb[:, :qk], "wk": wb[:, qk:2 * qk], "wv": wb[:, 2 * qk:2 * qk + vw],
        "wo": wb[:, 2 * qk + vw:2 * qk + 2 * vw],
        "wqt": wt[:qk], "wkt": wt[qk:2 * qk], "wvt": wt[2 * qk:2 * qk + vw],
        "wot": wt[2 * qk + vw:2 * qk + 2 * vw],
        "wg": jnp.pad(gates, ((0, 0), (0, LANES - 2 * h))),
        "wgt": wt[2 * qk + 2 * vw:],
        "brow": jnp.pad(bias, (0, LANES - 2 * h))[None, :],
        "bcol": jnp.broadcast_to(bias[:, None], (2 * h, LANES)),
    }


def _mlstm_scan_body(qt_ref, k_ref, vt_ref, ot_ref, gc_ref, gr_ref, ghn_ref,
                     out_ref, cfin_ref, nfin_ref, mfin_ref, c_s, n_s, m_s, *, t, dk, dv):
    c_idx = pl.program_id(1)

    @pl.when(c_idx == 0)
    def _():
        c_s[...] = jnp.zeros_like(c_s)
        n_s[...] = jnp.zeros_like(n_s)
        m_s[...] = jnp.zeros_like(m_s)

    row = lax.broadcasted_iota(jnp.int32, (t, t), 0)
    col = lax.broadcasted_iota(jnp.int32, (t, t), 1)
    src_le_tgt = row <= col
    tri_l = (col <= row).astype(BF16)
    tri_u = src_le_tgt.astype(BF16)
    gc = gc_ref[...]
    gr = gr_ref[...]
    bcol_all = _dot3(tri_l, gc, split_lhs=False)
    brow_all = _dot3(gr, tri_u, split_lhs=True)

    for h in range(M_HEADS):
        b_col = bcol_all[:, M_HEADS + h:M_HEADS + h + 1]
        b_row = brow_all[M_HEADS + h:M_HEADS + h + 1, :]
        u_col = gc[:, h:h + 1] - b_col
        u_row = gr[h:h + 1, :] - b_row
        m_prev = m_s[h:h + 1, 0:1]
        qth = qt_ref[h * dk:(h + 1) * dk, :]
        kh = k_ref[:, h * dk:(h + 1) * dk]
        vth = vt_ref[h * dv:(h + 1) * dv, :]
        ch = c_s[h]
        nh = n_s[h:h + 1, :]

        dmat = jnp.where(src_le_tgt, u_col + b_row, -jnp.inf)
        inter = b_row + m_prev
        mt = jnp.maximum(jnp.max(dmat, axis=0, keepdims=True), inter)
        a = jnp.exp(inter - mt)
        s = _dot(kh, qth) * jnp.exp(dmat - mt)
        num = _dot(vth, s.astype(BF16)) + a * _dot(ch.astype(BF16), qth)
        nq = _dot(jnp.broadcast_to(nh, (8, dk)).astype(BF16), qth)[0:1]
        den = jnp.sum(s, axis=0, keepdims=True) + a * nq
        hh = num / jnp.maximum(jnp.abs(den), jnp.exp(-mt))
        hn = hh * lax.rsqrt(jnp.mean(hh * hh, axis=0, keepdims=True) + EPS) * ghn_ref[:, h:h + 1]
        gated = jax.nn.sigmoid(ot_ref[h * dv:(h + 1) * dv, :]) * hn
        out_ref[h * dv:(h + 1) * dv, :] = gated.astype(BF16)

        bl = b_col[t - 1:t, :]
        g_row = bl + u_row
        m_new = jnp.maximum(bl + m_prev, jnp.max(g_row, axis=1, keepdims=True))
        wg_row = jnp.exp(g_row - m_new)
        ac = jnp.exp(bl + m_prev - m_new)
        vw = (vth.astype(F32) * wg_row).astype(BF16)
        c_s[h] = ac * ch + _dot(vw, kh)
        n_s[h:h + 1, :] = ac * nh + _dot(wg_row.astype(BF16), kh)
        m_s[h:h + 1, :] = jnp.broadcast_to(m_new, (1, LANES))

    @pl.when(c_idx == pl.num_programs(1) - 1)
    def _():
        cfin_ref[0] = c_s[...]
        nfin_ref[0] = n_s[0:M_HEADS, :]
        mfin_ref[0] = m_s[...]


def _mlstm_scan(qt, k, vt, ot, gc, gr, ghn, batch, t):
    n, qk = k.shape
    vw = vt.shape[0]
    dk, dv = qk // M_HEADS, vw // M_HEADS
    nc = n // batch // t
    row = lambda width: pl.BlockSpec((t, width), lambda b, c: (b * nc + c, 0))
    tspec = lambda rows: pl.BlockSpec((rows, t), lambda b, c: (0, b * nc + c))
    ghn_cols = jnp.pad(ghn.T, ((0, 0), (0, LANES - M_HEADS)))
    return pl.pallas_call(
        functools.partial(_mlstm_scan_body, t=t, dk=dk, dv=dv),
        grid=(batch, nc),
        in_specs=[tspec(qk), row(qk), tspec(vw), tspec(vw), row(LANES), tspec(8),
                  _const_spec((dv, LANES))],
        out_specs=[tspec(vw),
                   pl.BlockSpec((1, M_HEADS, dv, dk), lambda b, c: (b, 0, 0, 0)),
                   pl.BlockSpec((1, M_HEADS, dk), lambda b, c: (b, 0, 0)),
                   pl.BlockSpec((1, 8, LANES), lambda b, c: (b, 0, 0))],
        out_shape=[jax.ShapeDtypeStruct((vw, n), BF16),
                   jax.ShapeDtypeStruct((batch, M_HEADS, dv, dk), F32),
                   jax.ShapeDtypeStruct((batch, M_HEADS, dk), F32),
                   jax.ShapeDtypeStruct((batch, 8, LANES), F32)],
        scratch_shapes=[pltpu.VMEM((M_HEADS, dv, dk), F32), pltpu.VMEM((8, dk), F32),
                        pltpu.VMEM((8, LANES), F32)],
        compiler_params=_params("arbitrary", "arbitrary"),
        name="mlstm_scan",
    )(qt, k, vt, ot, gc, gr, ghn_cols)


def _mlstm_step_body(q_ref, k_ref, v_ref, o_ref, gc_ref, ghn_ref, c_ref, n_ref, m_ref,
                     out_ref, cn_ref, nn_ref, mn_ref, *, sb, dk, dv):
    gates = gc_ref[...]
    m_all = m_ref[...]
    q_all = q_ref[...].astype(F32)
    k_all = k_ref[...].astype(F32)
    v_all = v_ref[...].astype(F32)
    sig_o = jax.nn.sigmoid(o_ref[...])
    sub = lax.broadcasted_iota(jnp.int32, (8, dv), 0)
    rows = []
    for s in range(sb):
        outs = []
        for h in range(M_HEADS):
            li = gates[s:s + 1, h:h + 1]
            lf = gates[s:s + 1, M_HEADS + h:M_HEADS + h + 1]
            m_prev = m_all[s:s + 1, h:h + 1]
            qf = q_all[s:s + 1, h * dk:(h + 1) * dk]
            kf = k_all[s:s + 1, h * dk:(h + 1) * dk]
            vf = v_all[s:s + 1, h * dv:(h + 1) * dv]
            ch = c_ref[s, h]
            nh = n_ref[s, h:h + 1, :]

            inter = lf + m_prev
            mt = jnp.maximum(li, inter)
            w = jnp.exp(li - mt)
            a = jnp.exp(inter - mt)
            sc = jnp.sum(qf * kf, axis=1, keepdims=True) * w
            cq = _dot_nt(jnp.broadcast_to(qf, (8, dk)).astype(BF16), ch.astype(BF16))[0:1]
            num = sc * vf + a * cq
            den = sc + a * jnp.sum(nh * qf, axis=1, keepdims=True)
            hh = num / jnp.maximum(jnp.abs(den), jnp.exp(-mt))
            hn = hh * lax.rsqrt(jnp.mean(hh * hh, axis=1, keepdims=True) + EPS) * ghn_ref[h:h + 1, :]
            outs.append(sig_o[s:s + 1, h * dv:(h + 1) * dv] * hn)

            vw8 = jnp.where(sub == 0, jnp.broadcast_to(vf * w, (8, dv)), 0.0).astype(BF16)
            k8 = jnp.broadcast_to(kf, (8, dk)).astype(BF16)
            cn_ref[s, h] = a * ch + _dot_tn(vw8, k8)
            nn_ref[s, h:h + 1, :] = a * nh + w * kf
            mn_ref[s:s + 1, h:h + 1] = mt
        rows.append(jnp.concatenate(outs, axis=1))
    out_ref[...] = jnp.concatenate(rows, axis=0).astype(BF16)


N_MLSTM_STEP_IN = 9


def _mlstm_step(q, k, v, o, gc, ghn, c_all, n_all, m0, layer, sb, c_prev):
    n, qk = q.shape
    vw = v.shape[1]
    dk, dv = qk // M_HEADS, vw // M_HEADS
    flat = lambda a: a.reshape((-1,) + a.shape[2:])
    off = layer * (n // sb)
    row = lambda width: pl.BlockSpec((sb, width), lambda i: (i, 0))
    cspec = pl.BlockSpec((sb, M_HEADS, dv, dk), lambda i: (off + i, 0, 0, 0))
    nshape = (sb, M_HEADS, dk)
    body = functools.partial(_mlstm_step_body, sb=sb, dk=dk, dv=dv)
    operands = [q, k, v, o, gc, ghn, flat(c_all), flat(n_all), m0]
    in_specs = [row(qk), row(qk), row(vw), row(vw), row(LANES), _const_spec((M_HEADS, dv)),
                cspec, pl.BlockSpec(nshape, lambda i: (off + i, 0, 0)), row(M_HEADS)]
    aliases = {}
    if c_prev is not None:
        operands.append(flat(c_prev))
        in_specs.append(pl.BlockSpec(memory_space=pl.ANY))
        aliases = {N_MLSTM_STEP_IN: 1}
        inner = body

        def body(*refs):
            inner(*refs[:N_MLSTM_STEP_IN], *refs[N_MLSTM_STEP_IN + 1:])
    out, c_new, n_new, m_new = pl.pallas_call(
        body,
        grid=(n // sb,),
        in_specs=in_specs,
        out_specs=[row(vw), cspec, pl.BlockSpec(nshape, lambda i: (i, 0, 0)), row(M_HEADS)],
        out_shape=[jax.ShapeDtypeStruct((n, vw), BF16),
                   jax.ShapeDtypeStruct(flat(c_all).shape, F32), jax.ShapeDtypeStruct(n_all.shape[1:], F32),
                   jax.ShapeDtypeStruct(m0.shape, F32)],
        input_output_aliases=aliases,
        compiler_params=_params("parallel"),
        name="mlstm_step",
    )(*operands)
    return out, c_new.reshape(c_all.shape), n_new, m_new


def _mix_ffn_body(x_ref, a_ref, wout_ref, g_ref, wup_ref, wdn_ref, gfin_ref, o_ref, *, n_chunks, final, a_transposed):
    mixed = _dot_tn(a_ref[...], wout_ref[...]) if a_transposed else _dot(a_ref[...], wout_ref[...])
    x1 = x_ref[...] + mixed
    xn = _rmsnorm(x1, g_ref[...]).astype(BF16)
    fc = wup_ref.shape[1] // n_chunks
    acc = x1
    for c in range(n_chunks):
        hid = jnp.maximum(_dot(xn, wup_ref[:, c * fc:(c + 1) * fc]), 0.0)
        acc = acc + _dot((hid * hid).astype(BF16), wdn_ref[c * fc:(c + 1) * fc, :])
    o_ref[...] = _rmsnorm(acc, gfin_ref[...]) if final else acc


def _mix_ffn(x, a, w_out, g, w_up, w_down, g_final, tm, final, a_transposed=False):
    n, d = x.shape
    dff = w_up.shape[1]
    row = lambda width: pl.BlockSpec((tm, width), lambda i: (i, 0))
    a_spec = pl.BlockSpec((d, tm), lambda i: (0, i)) if a_transposed else row(d)
    return pl.pallas_call(
        functools.partial(_mix_ffn_body, n_chunks=4, final=final, a_transposed=a_transposed),
        grid=(n // tm,),
        in_specs=[row(d), a_spec, _const_spec((d, d)), _const_spec((1, d)),
                  _const_spec((d, dff)), _const_spec((dff, d)), _const_spec((1, d))],
        out_specs=row(d),
        out_shape=jax.ShapeDtypeStruct((n, d), F32),
        compiler_params=_params("parallel"),
        name="mix_ffn",
    )(x, a, w_out, g, w_up, w_down, g_final)


N_SPLIT = 3


def _bias_placement(hd):
    rows = jnp.arange(N_SPLIT * LANES)
    piece, h = rows // LANES, rows % LANES
    target = jnp.where(h < F_HEADS, h * LANES + hd + piece, -1)
    return -(target[:, None] == jnp.arange(F_HEADS * LANES)[None, :]).astype(BF16)


def _fox_weights(w_in, b_f):
    d = w_in.shape[0]
    hd = d // F_HEADS
    wt = w_in.T.astype(BF16)
    wkt, wft = wt[d:2 * d], wt[3 * d:]
    bias = b_f.astype(F32)
    wkt_slots = jnp.pad(wkt.reshape(F_HEADS, hd, d), ((0, 0), (0, LANES - hd), (0, 0))).reshape(F_HEADS * LANES, d)
    return {
        "wqt": wt[:d], "wkt": wkt, "wkt_slots": wkt_slots, "wvt": wt[2 * d:3 * d], "wft": wft,
        "wf": jnp.pad(wft.T, ((0, 0), (0, LANES - F_HEADS))),
        "bfrow": jnp.pad(bias, (0, LANES - F_HEADS))[None, :],
        "bfcol": jnp.broadcast_to(bias[:, None], (F_HEADS, LANES)),
        "place": _bias_placement(hd),
    }


def _fox_proj_body(x_ref, g_ref, wqt_ref, wkt_ref, wks_ref, wvt_ref, wf_ref, wft_ref, bfrow_ref, bfcol_ref, place_ref,
                   qt_ref, ka_ref, k32t_ref, vta_ref, v32t_ref, lft_ref, crow_ref,
                   carry_row_s, carry_col_s, *, q_scale, hd):
    @pl.when(pl.program_id(1) == 0)
    def _():
        carry_row_s[...] = jnp.zeros_like(carry_row_s)
        carry_col_s[...] = jnp.zeros_like(carry_col_s)

    xn = _rmsnorm(x_ref[...], g_ref[...]).astype(BF16)
    tm, d = xn.shape
    qt_ref[0] = (_dot_nt(wqt_ref[...], xn) * q_scale).astype(BF16)
    k32t_ref[0, 0] = _dot_nt(wkt_ref[...], xn)
    vt = _dot_nt(wvt_ref[...], xn)
    v32t_ref[0, 0] = vt
    sub = lax.broadcasted_iota(jnp.int32, (F_HEADS, LANES - hd, tm), 1)
    ones_row = jnp.where(sub == 0, 1.0, 0.0)
    vta = jnp.concatenate([vt.reshape(F_HEADS, hd, tm), ones_row], axis=1)
    vta_ref[0, 0] = vta.reshape(F_HEADS * LANES, tm).astype(BF16)

    zc = _dot(xn, wf_ref[...]) + bfrow_ref[...]
    lane = lax.broadcasted_iota(jnp.int32, zc.shape, 1)
    lfc = jnp.where(lane < F_HEADS, _log_sigmoid(zc), 0.0)
    lfr = _log_sigmoid(_dot_nt(wft_ref[...], xn) + bfcol_ref[:, 0:1])
    lft_ref[0] = lfr
    row = lax.broadcasted_iota(jnp.int32, (tm, tm), 0)
    col = lax.broadcasted_iota(jnp.int32, (tm, tm), 1)
    ccol = _dot3((col <= row).astype(BF16), lfc, split_lhs=False) + carry_row_s[...]
    carry_row_s[...] = ccol[tm - 1:tm, :]
    crow = _dot3(lfr, (row <= col).astype(BF16), split_lhs=True) + carry_col_s[:, 0:1]
    carry_col_s[...] = jnp.broadcast_to(crow[:, tm - 1:tm], carry_col_s.shape)
    crow_ref[0] = crow * LOG2E
    pieces = jnp.concatenate(_split3(ccol * LOG2E), axis=1)
    ka_ref[...] = (_dot_nt(xn, wks_ref[...]) + _dot(pieces, place_ref[...])).astype(BF16)


N_FOX_PROJ_IN = 11


def _fox_proj(x, g, w, batch, tm, slab, n_slabs, kv_prev):
    n, d = x.shape
    hd = d // F_HEADS
    seq = n // batch
    ns = seq // tm
    slots = F_HEADS * LANES
    row = lambda width: pl.BlockSpec((tm, width), lambda b, i: (b * ns + i, 0))
    tspec = lambda rows: pl.BlockSpec((1, rows, tm), lambda b, i: (b, 0, i))
    slab_spec = pl.BlockSpec((1, 1, d, tm), lambda b, i: (slab, b, 0, i))
    slab_shape = jax.ShapeDtypeStruct((n_slabs, batch, d, seq), F32)
    body = functools.partial(_fox_proj_body, q_scale=hd ** -0.5 * LOG2E, hd=hd)
    operands = [x, g, w["wqt"], w["wkt"], w["wkt_slots"], w["wvt"], w["wf"], w["wft"], w["bfrow"], w["bfcol"], w["place"]]
    in_specs = [row(d), _const_spec((1, d)), _const_spec((d, d)), _const_spec((d, d)), _const_spec((slots, d)),
                _const_spec((d, d)), _const_spec((d, LANES)), _const_spec((F_HEADS, d)), _const_spec((1, LANES)),
                _const_spec((F_HEADS, LANES)), _const_spec((N_SPLIT * LANES, slots))]
    aliases = {}
    if kv_prev is not None:
        operands += list(kv_prev)
        in_specs += [pl.BlockSpec(memory_space=pl.ANY)] * 2
        aliases = {N_FOX_PROJ_IN: 2, N_FOX_PROJ_IN + 1: 4}
        inner = body

        def body(*refs):
            inner(*refs[:N_FOX_PROJ_IN], *refs[N_FOX_PROJ_IN + 2:])
    return pl.pallas_call(
        body,
        grid=(batch, ns),
        in_specs=in_specs,
        out_specs=[tspec(d), row(slots), slab_spec,
                   pl.BlockSpec((1, 1, slots, tm), lambda b, i: (b, i, 0, 0)), slab_spec,
                   tspec(F_HEADS), tspec(F_HEADS)],
        out_shape=[jax.ShapeDtypeStruct((batch, d, seq), BF16), jax.ShapeDtypeStruct((n, slots), BF16), slab_shape,
                   jax.ShapeDtypeStruct((batch, ns, slots, tm), BF16), slab_shape,
                   jax.ShapeDtypeStruct((batch, F_HEADS, seq), F32), jax.ShapeDtypeStruct((batch, F_HEADS, seq), F32)],
        scratch_shapes=[pltpu.VMEM((1, LANES), F32), pltpu.VMEM((F_HEADS, LANES), F32)],
        input_output_aliases=aliases,
        compiler_params=_params("arbitrary", "arbitrary"),
        name="fox_proj",
    )(*operands)


def _fox_attn_body(qt_ref, ka_ref, vta_ref, crow_ref, o_ref, s_s, m_s, acc_s, *, tq, tk, hd):
    pair = pl.program_id(1)
    qi = pl.program_id(2)
    sub = lax.broadcasted_iota(jnp.int32, (LANES - hd, tq), 0)
    ones_rows = jnp.where(sub < N_SPLIT, 1.0, 0.0).astype(BF16)
    qa = [jnp.concatenate([qt_ref[0, h * hd:(h + 1) * hd, :], ones_rows], axis=0) for h in range(2)]
    cq = [crow_ref[0, pl.ds(2 * pair + h, 1), :] for h in range(2)]

    def scores(j, slot, heads=(0, 1)):
        start = pl.multiple_of(j * tk, tk)
        for h in heads:
            s_s[slot, h] = _dot(ka_ref[0, pl.ds(start, tk), h * LANES:(h + 1) * LANES], qa[h])

    def update(j, slot, masked, heads=(0, 1)):
        for h in heads:
            t = s_s[slot, h]
            if masked:
                krow = lax.broadcasted_iota(jnp.int32, (tk, tq), 0)
                qcol = lax.broadcasted_iota(jnp.int32, (tk, tq), 1)
                t = jnp.where(krow <= qcol, t, -jnp.inf)
            m = m_s[h]
            m_new = jnp.maximum(m, jnp.max(t, axis=0, keepdims=True) + cq[h])
            p = jnp.exp2(t + (cq[h] - m_new))
            pv = _dot(vta_ref[0, j, h * LANES:(h + 1) * LANES, :], p.astype(BF16))
            acc_s[h] = jnp.exp2(m - m_new) * acc_s[h] + pv
            m_s[h] = m_new

    def finish():
        ot = jnp.concatenate([acc_s[h, :hd, :] / acc_s[h, hd:hd + 1, :] for h in range(2)], axis=0)
        o_ref[0] = ot.T.astype(BF16)

    m_s[...] = jnp.full(m_s.shape, -jnp.inf, F32)
    acc_s[...] = jnp.zeros_like(acc_s)
    scores(0, 0)

    def two_blocks(j):
        for h in range(2):
            scores(j + 1, 1, (h,))
            update(j, 0, False, (h,))
        for h in range(2):
            scores(j + 2, 0, (h,))
            update(j + 1, 1, False, (h,))

    def body4(i, _):
        two_blocks(4 * i)
        two_blocks(4 * i + 2)
        return 0

    def body2(i, _):
        two_blocks(2 * i)
        return 0

    lax.fori_loop(0, qi // 4, body4, 0)
    lax.fori_loop(2 * (qi // 4), qi // 2, body2, 0)

    @pl.when(qi % 2 == 0)
    def _():
        update(qi, 0, True)
        finish()

    @pl.when(qi % 2 == 1)
    def _():
        for h in range(2):
            scores(qi, 1, (h,))
            update(qi - 1, 0, False, (h,))
        update(qi, 1, True)
        finish()


def _fox_attn(qt, ka, vta, crow, tile):
    batch, d, seq = qt.shape
    hd = d // F_HEADS
    nq = seq // tile
    slots = F_HEADS * LANES
    out = pl.pallas_call(
        functools.partial(_fox_attn_body, tq=tile, tk=tile, hd=hd),
        grid=(batch, F_HEADS // 2, nq),
        in_specs=[pl.BlockSpec((1, 2 * hd, tile), lambda b, p, i: (b, p, i)),
                  pl.BlockSpec((1, seq, 2 * LANES), lambda b, p, i: (b, 0, p)),
                  pl.BlockSpec((1, nq, 2 * LANES, tile), lambda b, p, i: (b, 0, p, 0)),
                  pl.BlockSpec((1, F_HEADS, tile), lambda b, p, i: (b, 0, i))],
        out_specs=pl.BlockSpec((1, tile, 2 * hd), lambda b, p, i: (b, i, p)),
        out_shape=jax.ShapeDtypeStruct((batch, seq, d), BF16),
        scratch_shapes=[pltpu.VMEM((2, 2, tile, tile), F32), pltpu.VMEM((2, 1, tile), F32),
                        pltpu.VMEM((2, LANES, tile), F32)],
        compiler_params=_params("parallel", "parallel", "arbitrary"),
        name="fox_attn",
    )(qt, ka.reshape(batch, seq, slots), vta, crow)
    return out.reshape(batch * seq, d)


def _fox_proj_sample_body(x_ref, g_ref, wqt_ref, wkt_ref, wvt_ref, wft_ref, bfcol_ref,
                          qt_ref, kt_ref, vt_ref, lft_ref, *, q_scale):
    xn = _rmsnorm(x_ref[...], g_ref[...]).astype(BF16)
    qt_ref[...] = _dot_nt(wqt_ref[...], xn) * q_scale
    kt_ref[...] = _dot_nt(wkt_ref[...], xn)
    vt_ref[...] = _dot_nt(wvt_ref[...], xn)
    lft_ref[...] = _log_sigmoid(_dot_nt(wft_ref[...], xn) + bfcol_ref[:, 0:1])


def _fox_proj_sample(x, g, w):
    n, d = x.shape
    hd = d // F_HEADS
    return pl.pallas_call(
        functools.partial(_fox_proj_sample_body, q_scale=hd ** -0.5),
        grid=(1,),
        in_specs=[_const_spec((n, d)), _const_spec((1, d)), _const_spec((d, d)), _const_spec((d, d)),
                  _const_spec((d, d)), _const_spec((F_HEADS, d)), _const_spec((F_HEADS, LANES))],
        out_specs=[pl.BlockSpec((d, n), lambda i: (0, 0))] * 3 + [pl.BlockSpec((F_HEADS, n), lambda i: (0, 0))],
        out_shape=[jax.ShapeDtypeStruct((d, n), F32)] * 3 + [jax.ShapeDtypeStruct((F_HEADS, n), F32)],
        compiler_params=_params("arbitrary"),
        name="fox_proj_sample",
    )(x, g, w["wqt"], w["wkt"], w["wvt"], w["wft"], w["bfcol"])


def _fox_decode_body(pt_ref, qt_ref, kct_ref, vct_ref, lfct_ref, *rest, n_pages, hd):
    del pt_ref
    k_refs = rest[:n_pages]
    v_refs = rest[n_pages:2 * n_pages]
    lf_refs = rest[2 * n_pages:3 * n_pages]
    o_ref, ot_s = rest[3 * n_pages:]
    b = pl.program_id(0)
    d, nb = qt_ref.shape
    heads = d // hd

    @pl.when(b == 0)
    def _():
        ot_s[...] = jnp.zeros_like(ot_s)

    def head_sum(x):
        return jnp.sum(x.reshape(heads, hd, x.shape[1]), axis=1)

    def head_rows(x):
        return jnp.broadcast_to(x[:, None, :], (heads, hd, x.shape[1])).reshape(d, x.shape[1])

    pick = lax.broadcasted_iota(jnp.int32, (d, nb), 1) == b
    column = lambda ref: jnp.sum(jnp.where(pick, ref[...], 0.0), axis=1, keepdims=True)
    qcol, kcol, vcol = column(qt_ref), column(kct_ref), column(vct_ref)
    pick_h = lax.broadcasted_iota(jnp.int32, (heads, nb), 1) == b
    lf_cur = jnp.sum(jnp.where(pick_h, lfct_ref[...], 0.0), axis=1, keepdims=True)

    ri = lax.broadcasted_iota(jnp.int32, (PAGE_SIZE, PAGE_SIZE), 0)
    ci = lax.broadcasted_iota(jnp.int32, (PAGE_SIZE, PAGE_SIZE), 1)
    later = (ri > ci).astype(BF16)

    suffix = lf_cur
    logits = [None] * n_pages
    for pg in reversed(range(n_pages)):
        lf = lf_refs[pg][0]
        bias = suffix + _dot3(lf, later, split_lhs=True)
        suffix = suffix + jnp.sum(lf, axis=1, keepdims=True)
        logits[pg] = head_sum(k_refs[pg][0] * qcol) + bias

    s_cur = head_sum(qcol * kcol)
    m = s_cur
    for pg in range(n_pages):
        m = jnp.maximum(m, jnp.max(logits[pg], axis=1, keepdims=True))
    p_cur = jnp.exp(s_cur - m)
    l = p_cur
    acc = jnp.zeros((d, PAGE_SIZE), F32)
    for pg in range(n_pages):
        p = jnp.exp(logits[pg] - m)
        l = l + jnp.sum(p, axis=1, keepdims=True)
        acc = acc + v_refs[pg][0] * head_rows(p)
    out = (jnp.sum(acc, axis=1, keepdims=True) + vcol * head_rows(p_cur)) / head_rows(l)
    ot_s[...] = jnp.where(pick, out, ot_s[...])

    @pl.when(b == nb - 1)
    def _():
        o_ref[...] = ot_s[...].T.astype(BF16)


def _fox_decode(qt, kct, vct, lfct, k_cache, v_cache, lf_cache, page_table, layer):
    d, n = qt.shape
    hd = d // F_HEADS
    n_pages = page_table.shape[1]
    n_pool = k_cache.shape[1]
    kp = jnp.transpose(k_cache, (0, 1, 3, 4, 2)).reshape(-1, d, PAGE_SIZE)
    vp = jnp.transpose(v_cache, (0, 1, 3, 4, 2)).reshape(-1, d, PAGE_SIZE)
    lfp = jnp.transpose(lf_cache, (0, 1, 3, 2)).reshape(-1, F_HEADS, PAGE_SIZE)
    off = layer * n_pool
    const = lambda rows: pl.BlockSpec((rows, n), lambda b, pt: (0, 0), pipeline_mode=pl.Buffered(1))
    page = lambda rows, pg: pl.BlockSpec((1, rows, PAGE_SIZE), lambda b, pt: (off + pt[b, pg], 0, 0))
    return pl.pallas_call(
        functools.partial(_fox_decode_body, n_pages=n_pages, hd=hd),
        grid_spec=pltpu.PrefetchScalarGridSpec(
            num_scalar_prefetch=1,
            grid=(n,),
            in_specs=[const(d), const(d), const(d), const(F_HEADS)]
                     + [page(d, pg) for pg in range(n_pages)] * 2
                     + [page(F_HEADS, pg) for pg in range(n_pages)],
            out_specs=pl.BlockSpec((n, d), lambda b, pt: (0, 0)),
            scratch_shapes=[pltpu.VMEM((d, n), F32)],
        ),
        out_shape=jax.ShapeDtypeStruct((n, d), BF16),
        compiler_params=_params("arbitrary"),
        name="fox_decode",
    )(page_table, qt, kct, vct, lfct, *([kp] * n_pages), *([vp] * n_pages), *([lfp] * n_pages))


def kernel(x_prompt, x_sample, state_mlstm_C, state_mlstm_n, state_mlstm_m, cache_fox_k, cache_fox_v, cache_fox_logf, page_table, norm_mix_g, norm_ffn_g, norm_final_g, mlstm_w_in, mlstm_b_i, mlstm_b_f, mlstm_g_hn, mlstm_w_out, fox_w_in, fox_b_f, fox_w_out, ffn_w_up, ffn_w_down):
    batch, seq, d = x_prompt.shape
    dec = x_sample.shape[0]
    depth = norm_mix_g.shape[0]
    hd = d // F_HEADS
    tm = min(ROW_TILE, seq)
    chunk = min(MLSTM_CHUNK, seq)
    xp = x_prompt.reshape(batch * seq, d)
    xs = x_sample.reshape(dec, d)
    g_final = norm_final_g.astype(F32)[None, :]
    c_all = state_mlstm_C.astype(F32)
    n_all = state_mlstm_n.astype(F32)

    mcp, mnp, mmp, mns, mms = [], [], [], [], []
    flp, fks, fvs, fls = [], [], [], []
    c_new = None
    kv32 = None
    for layer in range(depth):
        j = layer // 2
        g_mix = norm_mix_g[layer].astype(F32)[None, :]
        if layer % 2 == 0:
            w = _mlstm_weights(mlstm_w_in[j], mlstm_b_i[j], mlstm_b_f[j])
            ghn = mlstm_g_hn[j].astype(F32)
            qt, k, vt, ot, gc, gr = _mlstm_proj_t(xp, g_mix, w, tm)
            ap, c_p, n_p, m_p = _mlstm_scan(qt, k, vt, ot, gc, gr, ghn, batch, chunk)
            mcp.append(c_p); mnp.append(n_p); mmp.append(m_p[:, :M_HEADS, 0])
            q, k, v, o, gc, _ = _mlstm_proj(xs, g_mix, w, dec)
            as_, c_new, n_s, m_s = _mlstm_step(q, k, v, o, gc, ghn, c_all, n_all, state_mlstm_m[j].astype(F32),
                                               j, min(DEC_SAMPLES, dec), c_new)
            mns.append(n_s); mms.append(m_s)
            w_out = mlstm_w_out[j].astype(BF16)
        else:
            w = _fox_weights(fox_w_in[j], fox_b_f[j])
            qt, ka, k32t, vta, v32t, lft, crow = _fox_proj(xp, g_mix, w, batch, tm, j, depth // 2, kv32)
            kv32 = (k32t, v32t)
            ap = _fox_attn(qt, ka, vta, crow, tm)
            flp.append(lft)
            qt, kt, vt, lft = _fox_proj_sample(xs, g_mix, w)
            as_ = _fox_decode(qt, kt, vt, lft, cache_fox_k, cache_fox_v, cache_fox_logf, page_table, j)
            fks.append(kt.reshape(1, F_HEADS, hd, dec)); fvs.append(vt.reshape(1, F_HEADS, hd, dec))
            fls.append(lft.reshape(1, F_HEADS, dec))
            w_out = fox_w_out[j].astype(BF16)
        g_ffn = norm_ffn_g[layer].astype(F32)[None, :]
        w_up = ffn_w_up[layer].astype(BF16)
        w_down = ffn_w_down[layer].astype(BF16)
        final = layer == depth - 1
        xp = _mix_ffn(xp, ap, w_out, g_ffn, w_up, w_down, g_final, tm, final, a_transposed=layer % 2 == 0)
        xs = _mix_ffn(xs, as_, w_out, g_ffn, w_up, w_down, g_final, dec, final)

    fkp, fvp = (a.reshape(a.shape[0], batch, F_HEADS, hd, seq) for a in kv32)
    return (xp.reshape(batch, seq, d), xs.reshape(dec, 1, d),
            jnp.stack(mcp), jnp.stack(mnp), jnp.stack(mmp),
            c_new, jnp.stack(mns), jnp.stack(mms),
            jnp.transpose(fkp, (0, 1, 4, 2, 3)), jnp.transpose(fvp, (0, 1, 4, 2, 3)),
            jnp.transpose(jnp.stack(flp), (0, 1, 3, 2)),
            jnp.transpose(jnp.stack(fks), (0, 4, 1, 2, 3)), jnp.transpose(jnp.stack(fvs), (0, 4, 1, 2, 3)),
            jnp.transpose(jnp.stack(fls), (0, 3, 1, 2)))
```

```python
import functools
import math

import jax
import jax.numpy as jnp
from jax import lax
from jax.experimental import pallas as pl
from jax.experimental.pallas import tpu as pltpu

F32 = jnp.float32
BF16 = jnp.bfloat16

EPS = 1e-6
GATE_CAP = 15.0
M_HEADS = 4
F_HEADS = 16
PAGE_SIZE = 128
LOG2E = math.log2(math.e)
LANES = 128
VMEM_LIMIT = 56 * 1024 * 1024

ROW_TILE = 512
MLSTM_CHUNK = 256
DEC_SAMPLES = 8


def _dot(a, b):
    return jnp.dot(a, b, preferred_element_type=F32)


def _dot_nt(a, b):
    return lax.dot_general(a, b, (((1,), (1,)), ((), ())), preferred_element_type=F32)


def _dot_tn(a, b):
    return lax.dot_general(a, b, (((0,), (0,)), ((), ())), preferred_element_type=F32)


def _split3(x):
    hi = x.astype(BF16)
    r = x - hi.astype(F32)
    mid = r.astype(BF16)
    lo = (r - mid.astype(F32)).astype(BF16)
    return hi, mid, lo


def _dot3(a, b, split_lhs):
    if split_lhs:
        hi, mid, lo = _split3(a)
        return _dot(hi, b) + _dot(mid, b) + _dot(lo, b)
    hi, mid, lo = _split3(b)
    return _dot(a, hi) + _dot(a, mid) + _dot(a, lo)


def _rmsnorm(x, g):
    return x * lax.rsqrt(jnp.mean(x * x, axis=-1, keepdims=True) + EPS) * g


def _softcap(z):
    return GATE_CAP * jnp.tanh(z / GATE_CAP)


def _log_sigmoid(x):
    return jnp.minimum(x, 0.0) - jnp.log1p(jnp.exp(-jnp.abs(x)))


def _params(*semantics):
    return pltpu.CompilerParams(dimension_semantics=semantics, vmem_limit_bytes=VMEM_LIMIT)


def _const_spec(shape):
    return pl.BlockSpec(shape, lambda *_: (0,) * len(shape), pipeline_mode=pl.Buffered(1))


def _mlstm_proj_body(x_ref, g_ref, wq_ref, wk_ref, wv_ref, wo_ref, wg_ref, wgt_ref, brow_ref, bcol_ref,
                     q_ref, k_ref, v_ref, o_ref, gc_ref, gr_ref, *, k_scale):
    xn = _rmsnorm(x_ref[...], g_ref[...]).astype(BF16)
    q_ref[...] = _dot(xn, wq_ref[...]).astype(BF16)
    k_ref[...] = (_dot(xn, wk_ref[...]) * k_scale).astype(BF16)
    v_ref[...] = _dot(xn, wv_ref[...]).astype(BF16)
    o_ref[...] = _dot(xn, wo_ref[...])
    zc = _softcap(_dot(xn, wg_ref[...]) + brow_ref[...])
    lane = lax.broadcasted_iota(jnp.int32, zc.shape, 1)
    gc_ref[...] = jnp.where(lane < M_HEADS, zc, _log_sigmoid(zc))
    zr = _softcap(_dot_nt(wgt_ref[...], xn) + bcol_ref[:, 0:1])
    sub = lax.broadcasted_iota(jnp.int32, zr.shape, 0)
    gr_ref[...] = jnp.where(sub < M_HEADS, zr, _log_sigmoid(zr))


def _mlstm_proj(x, g, w, tm):
    n, d = x.shape
    qk, vw = w["wq"].shape[1], w["wv"].shape[1]
    dk = qk // M_HEADS
    row = lambda width: pl.BlockSpec((tm, width), lambda i: (i, 0))
    return pl.pallas_call(
        functools.partial(_mlstm_proj_body, k_scale=dk ** -0.5),
        grid=(n // tm,),
        in_specs=[row(d), _const_spec((1, d)), _const_spec((d, qk)), _const_spec((d, qk)),
                  _const_spec((d, vw)), _const_spec((d, vw)), _const_spec((d, LANES)),
                  _const_spec((8, d)), _const_spec((1, LANES)), _const_spec((8, LANES))],
        out_specs=[row(qk), row(qk), row(vw), row(vw), row(LANES),
                   pl.BlockSpec((8, tm), lambda i: (0, i))],
        out_shape=[jax.ShapeDtypeStruct((n, qk), BF16), jax.ShapeDtypeStruct((n, qk), BF16),
                   jax.ShapeDtypeStruct((n, vw), BF16), jax.ShapeDtypeStruct((n, vw), F32),
                   jax.ShapeDtypeStruct((n, LANES), F32), jax.ShapeDtypeStruct((8, n), F32)],
        compiler_params=_params("parallel"),
        name="mlstm_proj",
    )(x, g, w["wq"], w["wk"], w["wv"], w["wo"], w["wg"], w["wgt"], w["brow"], w["bcol"])


def _mlstm_proj_t_body(x_ref, g_ref, wqt_ref, wkt_ref, wvt_ref, wot_ref, wg_ref, wgt_ref, brow_ref, bcol_ref,
                       qt_ref, k_ref, vt_ref, ot_ref, gc_ref, gr_ref, *, k_scale):
    xn = _rmsnorm(x_ref[...], g_ref[...]).astype(BF16)
    qt_ref[...] = _dot_nt(wqt_ref[...], xn).astype(BF16)
    k_ref[...] = (_dot_nt(xn, wkt_ref[...]) * k_scale).astype(BF16)
    vt_ref[...] = _dot_nt(wvt_ref[...], xn).astype(BF16)
    ot_ref[...] = _dot_nt(wot_ref[...], xn)
    zc = _softcap(_dot(xn, wg_ref[...]) + brow_ref[...])
    lane = lax.broadcasted_iota(jnp.int32, zc.shape, 1)
    gc_ref[...] = jnp.where(lane < M_HEADS, zc, _log_sigmoid(zc))
    zr = _softcap(_dot_nt(wgt_ref[...], xn) + bcol_ref[:, 0:1])
    sub = lax.broadcasted_iota(jnp.int32, zr.shape, 0)
    gr_ref[...] = jnp.where(sub < M_HEADS, zr, _log_sigmoid(zr))


def _mlstm_proj_t(x, g, w, tm):
    n, d = x.shape
    qk, vw = w["wqt"].shape[0], w["wvt"].shape[0]
    dk = qk // M_HEADS
    row = lambda width: pl.BlockSpec((tm, width), lambda i: (i, 0))
    tspec = lambda rows: pl.BlockSpec((rows, tm), lambda i: (0, i))
    return pl.pallas_call(
        functools.partial(_mlstm_proj_t_body, k_scale=dk ** -0.5),
        grid=(n // tm,),
        in_specs=[row(d), _const_spec((1, d)), _const_spec((qk, d)), _const_spec((qk, d)),
                  _const_spec((vw, d)), _const_spec((vw, d)), _const_spec((d, LANES)),
                  _const_spec((8, d)), _const_spec((1, LANES)), _const_spec((8, LANES))],
        out_specs=[tspec(qk), row(qk), tspec(vw), tspec(vw), row(LANES), tspec(8)],
        out_shape=[jax.ShapeDtypeStruct((qk, n), BF16), jax.ShapeDtypeStruct((n, qk), BF16),
                   jax.ShapeDtypeStruct((vw, n), BF16), jax.ShapeDtypeStruct((vw, n), F32),
                   jax.ShapeDtypeStruct((n, LANES), F32), jax.ShapeDtypeStruct((8, n), F32)],
        compiler_params=_params("parallel"),
        name="mlstm_proj_t",
    )(x, g, w["wqt"], w["wkt"], w["wvt"], w["wot"], w["wg"], w["wgt"], w["brow"], w["bcol"])


def _mlstm_weights(w_in, b_i, b_f):
    d = w_in.shape[0]
    h = M_HEADS
    vw = d
    qk = (w_in.shape[1] - 2 * vw - 2 * h) // 2
    wb = w_in.astype(BF16)
    wt = w_in.T.astype(BF16)
    gates = wb[:, 2 * qk + 2 * vw:]
    bias = jnp.concatenate([b_i, b_f]).astype(F32)
    return {
        "wq": wb[:, :qk], "wk": wb[:, qk:2 * qk], "wv": wb[:, 2 * qk:2 * qk + vw],
        "wo": wb[:, 2 * qk + vw:2 * qk + 2 * vw],
        "wqt": wt[:qk], "wkt": wt[qk:2 * qk], "wvt": wt[2 * qk:2 * qk + vw],
        "wot": wt[2 * qk + vw:2 * qk + 2 * vw],
        "wg": jnp.pad(gates, ((0, 0), (0, LANES - 2 * h))),
        "wgt": wt[2 * qk + 2 * vw:],
        "brow": jnp.pad(bias, (0, LANES - 2 * h))[None, :],
        "bcol": jnp.broadcast_to(bias[:, None], (2 * h, LANES)),
    }


def _mlstm_scan_body(qt_ref, k_ref, vt_ref, ot_ref, gc_ref, gr_ref, ghn_ref,
                     out_ref, cfin_ref, nfin_ref, mfin_ref, c_s, n_s, m_s, *, t, dk, dv):
    c_idx = pl.program_id(1)

    @pl.when(c_idx == 0)
    def _():
        c_s[...] = jnp.zeros_like(c_s)
        n_s[...] = jnp.zeros_like(n_s)
        m_s[...] = jnp.zeros_like(m_s)

    row = lax.broadcasted_iota(jnp.int32, (t, t), 0)
    col = lax.broadcasted_iota(jnp.int32, (t, t), 1)
    src_le_tgt = row <= col
    tri_l = (col <= row).astype(BF16)
    tri_u = src_le_tgt.astype(BF16)
    gc = gc_ref[...]
    gr = gr_ref[...]
    bcol_all = _dot3(tri_l, gc, split_lhs=False)
    brow_all = _dot3(gr, tri_u, split_lhs=True)

    for h in range(M_HEADS):
        b_col = bcol_all[:, M_HEADS + h:M_HEADS + h + 1]
        b_row = brow_all[M_HEADS + h:M_HEADS + h + 1, :]
        u_col = gc[:, h:h + 1] - b_col
        u_row = gr[h:h + 1, :] - b_row
        m_prev = m_s[h:h + 1, 0:1]
        qth = qt_ref[h * dk:(h + 1) * dk, :]
        kh = k_ref[:, h * dk:(h + 1) * dk]
        vth = vt_ref[h * dv:(h + 1) * dv, :]
        ch = c_s[h]
        nh = n_s[h:h + 1, :]

        dmat = jnp.where(src_le_tgt, u_col + b_row, -jnp.inf)
        inter = b_row + m_prev
        mt = jnp.maximum(jnp.max(dmat, axis=0, keepdims=True), inter)
        a = jnp.exp(inter - mt)
        s = _dot(kh, qth) * jnp.exp(dmat - mt)
        num = _dot(vth, s.astype(BF16)) + a * _dot(ch.astype(BF16), qth)
        nq = _dot(jnp.broadcast_to(nh, (8, dk)).astype(BF16), qth)[0:1]
        den = jnp.sum(s, axis=0, keepdims=True) + a * nq
        hh = num / jnp.maximum(jnp.abs(den), jnp.exp(-mt))
        hn = hh * lax.rsqrt(jnp.mean(hh * hh, axis=0, keepdims=True) + EPS) * ghn_ref[:, h:h + 1]
        gated = jax.nn.sigmoid(ot_ref[h * dv:(h + 1) * dv, :]) * hn
        out_ref[h * dv:(h + 1) * dv, :] = gated.astype(BF16)

        bl = b_col[t - 1:t, :]
        g_row = bl + u_row
        m_new = jnp.maximum(bl + m_prev, jnp.max(g_row, axis=1, keepdims=True))
        wg_row = jnp.exp(g_row - m_new)
        ac = jnp.exp(bl + m_prev - m_new)
        vw = (vth.astype(F32) * wg_row).astype(BF16)
        c_s[h] = ac * ch + _dot(vw, kh)
        n_s[h:h + 1, :] = ac * nh + _dot(wg_row.astype(BF16), kh)
        m_s[h:h + 1, :] = jnp.broadcast_to(m_new, (1, LANES))

    @pl.when(c_idx == pl.num_programs(1) - 1)
    def _():
        cfin_ref[0] = c_s[...]
        nfin_ref[0] = n_s[0:M_HEADS, :]
        mfin_ref[0] = m_s[...]


def _mlstm_scan(qt, k, vt, ot, gc, gr, ghn, batch, t):
    n, qk = k.shape
    vw = vt.shape[0]
    dk, dv = qk // M_HEADS, vw // M_HEADS
    nc = n // batch // t
    row = lambda width: pl.BlockSpec((t, width), lambda b, c: (b * nc + c, 0))
    tspec = lambda rows: pl.BlockSpec((rows, t), lambda b, c: (0, b * nc + c))
    ghn_cols = jnp.pad(ghn.T, ((0, 0), (0, LANES - M_HEADS)))
    return pl.pallas_call(
        functools.partial(_mlstm_scan_body, t=t, dk=dk, dv=dv),
        grid=(batch, nc),
        in_specs=[tspec(qk), row(qk), tspec(vw), tspec(vw), row(LANES), tspec(8),
                  _const_spec((dv, LANES))],
        out_specs=[tspec(vw),
                   pl.BlockSpec((1, M_HEADS, dv, dk), lambda b, c: (b, 0, 0, 0)),
                   pl.BlockSpec((1, M_HEADS, dk), lambda b, c: (b, 0, 0)),
                   pl.BlockSpec((1, 8, LANES), lambda b, c: (b, 0, 0))],
        out_shape=[jax.ShapeDtypeStruct((vw, n), BF16),
                   jax.ShapeDtypeStruct((batch, M_HEADS, dv, dk), F32),
                   jax.ShapeDtypeStruct((batch, M_HEADS, dk), F32),
                   jax.ShapeDtypeStruct((batch, 8, LANES), F32)],
        scratch_shapes=[pltpu.VMEM((M_HEADS, dv, dk), F32), pltpu.VMEM((8, dk), F32),
                        pltpu.VMEM((8, LANES), F32)],
        compiler_params=_params("arbitrary", "arbitrary"),
        name="mlstm_scan",
    )(qt, k, vt, ot, gc, gr, ghn_cols)


def _mlstm_step_body(q_ref, k_ref, v_ref, o_ref, gc_ref, ghn_ref, c_ref, n_ref, m_ref,
                     out_ref, cn_ref, nn_ref, mn_ref, *, sb, dk, dv):
    gates = gc_ref[...]
    m_all = m_ref[...]
    q_all = q_ref[...].astype(F32)
    k_all = k_ref[...].astype(F32)
    v_all = v_ref[...].astype(F32)
    sig_o = jax.nn.sigmoid(o_ref[...])
    sub = lax.broadcasted_iota(jnp.int32, (8, dv), 0)
    rows = []
    for s in range(sb):
        outs = []
        for h in range(M_HEADS):
            li = gates[s:s + 1, h:h + 1]
            lf = gates[s:s + 1, M_HEADS + h:M_HEADS + h + 1]
            m_prev = m_all[s:s + 1, h:h + 1]
            qf = q_all[s:s + 1, h * dk:(h + 1) * dk]
            kf = k_all[s:s + 1, h * dk:(h + 1) * dk]
            vf = v_all[s:s + 1, h * dv:(h + 1) * dv]
            ch = c_ref[s, h]
            nh = n_ref[s, h:h + 1, :]

            inter = lf + m_prev
            mt = jnp.maximum(li, inter)
            w = jnp.exp(li - mt)
            a = jnp.exp(inter - mt)
            sc = jnp.sum(qf * kf, axis=1, keepdims=True) * w
            cq = _dot_nt(jnp.broadcast_to(qf, (8, dk)).astype(BF16), ch.astype(BF16))[0:1]
            num = sc * vf + a * cq
            den = sc + a * jnp.sum(nh * qf, axis=1, keepdims=True)
            hh = num / jnp.maximum(jnp.abs(den), jnp.exp(-mt))
            hn = hh * lax.rsqrt(jnp.mean(hh * hh, axis=1, keepdims=True) + EPS) * ghn_ref[h:h + 1, :]
            outs.append(sig_o[s:s + 1, h * dv:(h + 1) * dv] * hn)

            vw8 = jnp.where(sub == 0, jnp.broadcast_to(vf * w, (8, dv)), 0.0).astype(BF16)
            k8 = jnp.broadcast_to(kf, (8, dk)).astype(BF16)
            cn_ref[s, h] = a * ch + _dot_tn(vw8, k8)
            nn_ref[s, h:h + 1, :] = a * nh + w * kf
            mn_ref[s:s + 1, h:h + 1] = mt
        rows.append(jnp.concatenate(outs, axis=1))
    out_ref[...] = jnp.concatenate(rows, axis=0).astype(BF16)


N_MLSTM_STEP_IN = 9


def _mlstm_step(q, k, v, o, gc, ghn, c_all, n_all, m0, layer, sb, c_prev):
    n, qk = q.shape
    vw = v.shape[1]
    dk, dv = qk // M_HEADS, vw // M_HEADS
    flat = lambda a: a.reshape((-1,) + a.shape[2:])
    off = layer * (n // sb)
    row = lambda width: pl.BlockSpec((sb, width), lambda i: (i, 0))
    cspec = pl.BlockSpec((sb, M_HEADS, dv, dk), lambda i: (off + i, 0, 0, 0))
    nshape = (sb, M_HEADS, dk)
    body = functools.partial(_mlstm_step_body, sb=sb, dk=dk, dv=dv)
    operands = [q, k, v, o, gc, ghn, flat(c_all), flat(n_all), m0]
    in_specs = [row(qk), row(qk), row(vw), row(vw), row(LANES), _const_spec((M_HEADS, dv)),
                cspec, pl.BlockSpec(nshape, lambda i: (off + i, 0, 0)), row(M_HEADS)]
    aliases = {}
    if c_prev is not None:
        operands.append(flat(c_prev))
        in_specs.append(pl.BlockSpec(memory_space=pl.ANY))
        aliases = {N_MLSTM_STEP_IN: 1}
        inner = body

        def body(*refs):
            inner(*refs[:N_MLSTM_STEP_IN], *refs[N_MLSTM_STEP_IN + 1:])
    out, c_new, n_new, m_new = pl.pallas_call(
        body,
        grid=(n // sb,),
        in_specs=in_specs,
        out_specs=[row(vw), cspec, pl.BlockSpec(nshape, lambda i: (i, 0, 0)), row(M_HEADS)],
        out_shape=[jax.ShapeDtypeStruct((n, vw), BF16),
                   jax.ShapeDtypeStruct(flat(c_all).shape, F32), jax.ShapeDtypeStruct(n_all.shape[1:], F32),
                   jax.ShapeDtypeStruct(m0.shape, F32)],
        input_output_aliases=aliases,
        compiler_params=_params("parallel"),
        name="mlstm_step",
    )(*operands)
    return out, c_new.reshape(c_all.shape), n_new, m_new


def _mix_ffn_body(x_ref, a_ref, wout_ref, g_ref, wup_ref, wdn_ref, gfin_ref, o_ref, *, n_chunks, final, a_transposed):
    mixed = _dot_tn(a_ref[...], wout_ref[...]) if a_transposed else _dot(a_ref[...], wout_ref[...])
    x1 = x_ref[...] + mixed
    xn = _rmsnorm(x1, g_ref[...]).astype(BF16)
    fc = wup_ref.shape[1] // n_chunks
    acc = x1
    for c in range(n_chunks):
        hid = jnp.maximum(_dot(xn, wup_ref[:, c * fc:(c + 1) * fc]), 0.0)
        acc = acc + _dot((hid * hid).astype(BF16), wdn_ref[c * fc:(c + 1) * fc, :])
    o_ref[...] = _rmsnorm(acc, gfin_ref[...]) if final else acc


def _mix_ffn(x, a, w_out, g, w_up, w_down, g_final, tm, final, a_transposed=False):
    n, d = x.shape
    dff = w_up.shape[1]
    row = lambda width: pl.BlockSpec((tm, width), lambda i: (i, 0))
    a_spec = pl.BlockSpec((d, tm), lambda i: (0, i)) if a_transposed else row(d)
    return pl.pallas_call(
        functools.partial(_mix_ffn_body, n_chunks=4, final=final, a_transposed=a_transposed),
        grid=(n // tm,),
        in_specs=[row(d), a_spec, _const_spec((d, d)), _const_spec((1, d)),
                  _const_spec((d, dff)), _const_spec((dff, d)), _const_spec((1, d))],
        out_specs=row(d),
        out_shape=jax.ShapeDtypeStruct((n, d), F32),
        compiler_params=_params("parallel"),
        name="mix_ffn",
    )(x, a, w_out, g, w_up, w_down, g_final)


N_SPLIT = 3


def _bias_placement(hd):
    rows = jnp.arange(N_SPLIT * LANES)
    piece, h = rows // LANES, rows % LANES
    target = jnp.where(h < F_HEADS, h * LANES + hd + piece, -1)
    return -(target[:, None] == jnp.arange(F_HEADS * LANES)[None, :]).astype(BF16)


def _fox_weights(w_in, b_f):
    d = w_in.shape[0]
    hd = d // F_HEADS
    wt = w_in.T.astype(BF16)
    wkt, wft = wt[d:2 * d], wt[3 * d:]
    bias = b_f.astype(F32)
    wkt_slots = jnp.pad(wkt.reshape(F_HEADS, hd, d), ((0, 0), (0, LANES - hd), (0, 0))).reshape(F_HEADS * LANES, d)
    return {
        "wqt": wt[:d], "wkt": wkt, "wkt_slots": wkt_slots, "wvt": wt[2 * d:3 * d], "wft": wft,
        "wf": jnp.pad(wft.T, ((0, 0), (0, LANES - F_HEADS))),
        "bfrow": jnp.pad(bias, (0, LANES - F_HEADS))[None, :],
        "bfcol": jnp.broadcast_to(bias[:, None], (F_HEADS, LANES)),
        "place": _bias_placement(hd),
    }


def _fox_proj_body(x_ref, g_ref, wqt_ref, wkt_ref, wks_ref, wvt_ref, wf_ref, wft_ref, bfrow_ref, bfcol_ref, place_ref,
                   qt_ref, ka_ref, k32t_ref, vta_ref, v32t_ref, lft_ref, crow_ref,
                   carry_row_s, carry_col_s, *, q_scale, hd):
    @pl.when(pl.program_id(1) == 0)
    def _():
        carry_row_s[...] = jnp.zeros_like(carry_row_s)
        carry_col_s[...] = jnp.zeros_like(carry_col_s)

    xn = _rmsnorm(x_ref[...], g_ref[...]).astype(BF16)
    tm, d = xn.shape
    qt_ref[0] = (_dot_nt(wqt_ref[...], xn) * q_scale).astype(BF16)
    k32t_ref[0, 0] = _dot_nt(wkt_ref[...], xn)
    vt = _dot_nt(wvt_ref[...], xn)
    v32t_ref[0, 0] = vt
    sub = lax.broadcasted_iota(jnp.int32, (F_HEADS, LANES - hd, tm), 1)
    ones_row = jnp.where(sub == 0, 1.0, 0.0)
    vta = jnp.concatenate([vt.reshape(F_HEADS, hd, tm), ones_row], axis=1)
    vta_ref[0, 0] = vta.reshape(F_HEADS * LANES, tm).astype(BF16)

    zc = _dot(xn, wf_ref[...]) + bfrow_ref[...]
    lane = lax.broadcasted_iota(jnp.int32, zc.shape, 1)
    lfc = jnp.where(lane < F_HEADS, _log_sigmoid(zc), 0.0)
    lfr = _log_sigmoid(_dot_nt(wft_ref[...], xn) + bfcol_ref[:, 0:1])
    lft_ref[0] = lfr
    row = lax.broadcasted_iota(jnp.int32, (tm, tm), 0)
    col = lax.broadcasted_iota(jnp.int32, (tm, tm), 1)
    ccol = _dot3((col <= row).astype(BF16), lfc, split_lhs=False) + carry_row_s[...]
    carry_row_s[...] = ccol[tm - 1:tm, :]
    crow = _dot3(lfr, (row <= col).astype(BF16), split_lhs=True) + carry_col_s[:, 0:1]
    carry_col_s[...] = jnp.broadcast_to(crow[:, tm - 1:tm], carry_col_s.shape)
    crow_ref[0] = crow * LOG2E
    pieces = jnp.concatenate(_split3(ccol * LOG2E), axis=1)
    ka_ref[...] = (_dot_nt(xn, wks_ref[...]) + _dot(pieces, place_ref[...])).astype(BF16)


N_FOX_PROJ_IN = 11


def _fox_proj(x, g, w, batch, tm, slab, n_slabs, kv_prev):
    n, d = x.shape
    hd = d // F_HEADS
    seq = n // batch
    ns = seq // tm
    slots = F_HEADS * LANES
    row = lambda width: pl.BlockSpec((tm, width), lambda b, i: (b * ns + i, 0))
    tspec = lambda rows: pl.BlockSpec((1, rows, tm), lambda b, i: (b, 0, i))
    slab_spec = pl.BlockSpec((1, 1, d, tm), lambda b, i: (slab, b, 0, i))
    slab_shape = jax.ShapeDtypeStruct((n_slabs, batch, d, seq), F32)
    body = functools.partial(_fox_proj_body, q_scale=hd ** -0.5 * LOG2E, hd=hd)
    operands = [x, g, w["wqt"], w["wkt"], w["wkt_slots"], w["wvt"], w["wf"], w["wft"], w["bfrow"], w["bfcol"], w["place"]]
    in_specs = [row(d), _const_spec((1, d)), _const_spec((d, d)), _const_spec((d, d)), _const_spec((slots, d)),
                _const_spec((d, d)), _const_spec((d, LANES)), _const_spec((F_HEADS, d)), _const_spec((1, LANES)),
                _const_spec((F_HEADS, LANES)), _const_spec((N_SPLIT * LANES, slots))]
    aliases = {}
    if kv_prev is not None:
        operands += list(kv_prev)
        in_specs += [pl.BlockSpec(memory_space=pl.ANY)] * 2
        aliases = {N_FOX_PROJ_IN: 2, N_FOX_PROJ_IN + 1: 4}
        inner = body

        def body(*refs):
            inner(*refs[:N_FOX_PROJ_IN], *refs[N_FOX_PROJ_IN + 2:])
    return pl.pallas_call(
        body,
        grid=(batch, ns),
        in_specs=in_specs,
        out_specs=[tspec(d), row(slots), slab_spec,
                   pl.BlockSpec((1, 1, slots, tm), lambda b, i: (b, i, 0, 0)), slab_spec,
                   tspec(F_HEADS), tspec(F_HEADS)],
        out_shape=[jax.ShapeDtypeStruct((batch, d, seq), BF16), jax.ShapeDtypeStruct((n, slots), BF16), slab_shape,
                   jax.ShapeDtypeStruct((batch, ns, slots, tm), BF16), slab_shape,
                   jax.ShapeDtypeStruct((batch, F_HEADS, seq), F32), jax.ShapeDtypeStruct((batch, F_HEADS, seq), F32)],
        scratch_shapes=[pltpu.VMEM((1, LANES), F32), pltpu.VMEM((F_HEADS, LANES), F32)],
        input_output_aliases=aliases,
        compiler_params=_params("arbitrary", "arbitrary"),
        name="fox_proj",
    )(*operands)


def _attention_tile(qt_ref, ka_ref, vta_ref, crow_ref, o_ref, s_s, m_s, acc_s, *, tq, tk, hd, overlapped_work):
    pair = pl.program_id(1)
    qi = pl.program_id(2)
    sub = lax.broadcasted_iota(jnp.int32, (LANES - hd, tq), 0)
    ones_rows = jnp.where(sub < N_SPLIT, 1.0, 0.0).astype(BF16)
    qa = [jnp.concatenate([qt_ref[0, h * hd:(h + 1) * hd, :], ones_rows], axis=0) for h in range(2)]
    cq = [crow_ref[0, pl.ds(2 * pair + h, 1), :] for h in range(2)]

    def scores(j, slot, heads=(0, 1)):
        start = pl.multiple_of(j * tk, tk)
        for h in heads:
            s_s[slot, h] = _dot(ka_ref[0, pl.ds(start, tk), h * LANES:(h + 1) * LANES], qa[h])

    def update(j, slot, masked, heads=(0, 1)):
        for h in heads:
            t = s_s[slot, h]
            if masked:
                krow = lax.broadcasted_iota(jnp.int32, (tk, tq), 0)
                qcol = lax.broadcasted_iota(jnp.int32, (tk, tq), 1)
                t = jnp.where(krow <= qcol, t, -jnp.inf)
            m = m_s[h]
            m_new = jnp.maximum(m, jnp.max(t, axis=0, keepdims=True) + cq[h])
            p = jnp.exp2(t + (cq[h] - m_new))
            pv = _dot(vta_ref[0, j, h * LANES:(h + 1) * LANES, :], p.astype(BF16))
            acc_s[h] = jnp.exp2(m - m_new) * acc_s[h] + pv
            m_s[h] = m_new

    def finish():
        ot = jnp.concatenate([acc_s[h, :hd, :] / acc_s[h, hd:hd + 1, :] for h in range(2)], axis=0)
        o_ref[0] = ot.T.astype(BF16)

    m_s[...] = jnp.full(m_s.shape, -jnp.inf, F32)
    acc_s[...] = jnp.zeros_like(acc_s)
    scores(0, 0)
    overlapped_work()

    def two_blocks(j):
        for h in range(2):
            scores(j + 1, 1, (h,))
            update(j, 0, False, (h,))
        for h in range(2):
            scores(j + 2, 0, (h,))
            update(j + 1, 1, False, (h,))

    def body4(i, _):
        two_blocks(4 * i)
        two_blocks(4 * i + 2)
        return 0

    def body2(i, _):
        two_blocks(2 * i)
        return 0

    lax.fori_loop(0, qi // 4, body4, 0)
    lax.fori_loop(2 * (qi // 4), qi // 2, body2, 0)

    @pl.when(qi % 2 == 0)
    def _():
        update(qi, 0, True)
        finish()

    @pl.when(qi % 2 == 1)
    def _():
        for h in range(2):
            scores(qi, 1, (h,))
            update(qi - 1, 0, False, (h,))
        update(qi, 1, True)
        finish()


def _fox_proj_sample_body(x_ref, g_ref, wqt_ref, wkt_ref, wvt_ref, wft_ref, bfcol_ref,
                          qt_ref, kt_ref, vt_ref, lft_ref, *, q_scale):
    xn = _rmsnorm(x_ref[...], g_ref[...]).astype(BF16)
    qt_ref[...] = _dot_nt(wqt_ref[...], xn) * q_scale
    kt_ref[...] = _dot_nt(wkt_ref[...], xn)
    vt_ref[...] = _dot_nt(wvt_ref[...], xn)
    lft_ref[...] = _log_sigmoid(_dot_nt(wft_ref[...], xn) + bfcol_ref[:, 0:1])


def _fox_proj_sample(x, g, w):
    n, d = x.shape
    hd = d // F_HEADS
    return pl.pallas_call(
        functools.partial(_fox_proj_sample_body, q_scale=hd ** -0.5),
        grid=(1,),
        in_specs=[_const_spec((n, d)), _const_spec((1, d)), _const_spec((d, d)), _const_spec((d, d)),
                  _const_spec((d, d)), _const_spec((F_HEADS, d)), _const_spec((F_HEADS, LANES))],
        out_specs=[pl.BlockSpec((d, n), lambda i: (0, 0))] * 3 + [pl.BlockSpec((F_HEADS, n), lambda i: (0, 0))],
        out_shape=[jax.ShapeDtypeStruct((d, n), F32)] * 3 + [jax.ShapeDtypeStruct((F_HEADS, n), F32)],
        compiler_params=_params("arbitrary"),
        name="fox_proj_sample",
    )(x, g, w["wqt"], w["wkt"], w["wvt"], w["wft"], w["bfcol"])


def _decode_part(step, qt_ref, kct_ref, vct_ref, lfct_ref, k_refs, v_refs, lf_refs, od_ref,
                 ot_s, col_s, dm_s, dl_s, dsuf_s, dacc_s, *, hd, parts):
    d, nb = qt_ref.shape
    heads = d // hd
    sample = step // parts
    part = step % parts
    pick = lax.broadcasted_iota(jnp.int32, (d, nb), 1) == sample

    def head_sum(x):
        return jnp.sum(x.reshape(heads, hd, x.shape[1]), axis=1)

    def head_rows(x):
        return jnp.broadcast_to(x[:, None, :], (heads, hd, x.shape[1])).reshape(d, x.shape[1])

    def start():
        @pl.when(step == 0)
        def _():
            ot_s[...] = jnp.zeros_like(ot_s)

        @pl.when(part == 0)
        def _():
            for c, ref in enumerate((qt_ref, kct_ref, vct_ref)):
                col_s[c] = jnp.sum(jnp.where(pick, ref[...], 0.0), axis=1, keepdims=True)
            pick_h = lax.broadcasted_iota(jnp.int32, (heads, nb), 1) == sample
            dsuf_s[...] = jnp.sum(jnp.where(pick_h, lfct_ref[...], 0.0), axis=1, keepdims=True)
            dm_s[...] = head_sum(col_s[0] * col_s[1])
            dl_s[...] = jnp.ones_like(dl_s)
            dacc_s[...] = jnp.zeros_like(dacc_s)

    def main():
        qcol = col_s[0]
        ri = lax.broadcasted_iota(jnp.int32, (PAGE_SIZE, PAGE_SIZE), 0)
        ci = lax.broadcasted_iota(jnp.int32, (PAGE_SIZE, PAGE_SIZE), 1)
        later = (ri > ci).astype(BF16)
        suffix = dsuf_s[...]
        logits = [None] * len(k_refs)
        for pg in reversed(range(len(k_refs))):
            lf = lf_refs[pg][0]
            bias = suffix + _dot3(lf, later, split_lhs=True)
            suffix = suffix + jnp.sum(lf, axis=1, keepdims=True)
            logits[pg] = head_sum(k_refs[pg][0] * qcol) + bias
        dsuf_s[...] = suffix
        m_old = dm_s[...]
        m = m_old
        for lg in logits:
            m = jnp.maximum(m, jnp.max(lg, axis=1, keepdims=True))
        alpha = jnp.exp(m_old - m)
        l = alpha * dl_s[...]
        acc = head_rows(alpha) * dacc_s[...]
        for pg, lg in enumerate(logits):
            p = jnp.exp(lg - m)
            l = l + jnp.sum(p, axis=1, keepdims=True)
            acc = acc + v_refs[pg][0] * head_rows(p)
        dm_s[...] = m
        dl_s[...] = l
        dacc_s[...] = acc

    def finish():
        @pl.when(part == parts - 1)
        def _():
            p_cur = jnp.exp(head_sum(col_s[0] * col_s[1]) - dm_s[...])
            out = ((jnp.sum(dacc_s[...], axis=1, keepdims=True) + col_s[2] * head_rows(p_cur))
                   / head_rows(dl_s[...]))
            ot_s[...] = jnp.where(pick, out, ot_s[...])

        @pl.when(step == nb * parts - 1)
        def _():
            od_ref[...] = ot_s[...].T.astype(BF16)

    return start, main, finish


def _fox_attn_decode_body(pt_ref, qt_ref, ka_ref, vta_ref, crow_ref, qd_ref, kcd_ref, vcd_ref, lfcd_ref, *rest,
                          tq, tk, hd, ppp, parts):
    del pt_ref
    k_refs, v_refs, lf_refs = rest[:ppp], rest[ppp:2 * ppp], rest[2 * ppp:3 * ppp]
    o_ref, od_ref, s_s, m_s, acc_s, ot_s, col_s, dm_s, dl_s, dsuf_s, dacc_s = rest[3 * ppp:]
    step = (pl.program_id(0) * pl.num_programs(1) + pl.program_id(1)) * pl.num_programs(2) + pl.program_id(2)
    start, main, finish = _decode_part(step, qd_ref, kcd_ref, vcd_ref, lfcd_ref, k_refs, v_refs, lf_refs, od_ref,
                                       ot_s, col_s, dm_s, dl_s, dsuf_s, dacc_s, hd=hd, parts=parts)
    start()
    _attention_tile(qt_ref, ka_ref, vta_ref, crow_ref, o_ref, s_s, m_s, acc_s, tq=tq, tk=tk, hd=hd,
                    overlapped_work=main)
    finish()


def _fox_attn_decode(qt, ka, vta, crow, tile, qd, kcd, vcd, lfcd, k_cache, v_cache, lf_cache, page_table, layer):
    batch, d, seq = qt.shape
    hd = d // F_HEADS
    nq = seq // tile
    pairs = F_HEADS // 2
    slots = F_HEADS * LANES
    dec = qd.shape[1]
    n_pages = page_table.shape[1]
    n_pool = k_cache.shape[1]
    steps = batch * pairs * nq
    parts = steps // dec
    assert parts * dec == steps and n_pages % parts == 0, (steps, dec, n_pages)
    ppp = n_pages // parts
    kp = jnp.transpose(k_cache, (0, 1, 3, 4, 2)).reshape(-1, d, PAGE_SIZE)
    vp = jnp.transpose(v_cache, (0, 1, 3, 4, 2)).reshape(-1, d, PAGE_SIZE)
    lfp = jnp.transpose(lf_cache, (0, 1, 3, 2)).reshape(-1, F_HEADS, PAGE_SIZE)
    off = layer * n_pool

    def page(rows, k):
        def index(b, p, i, pt):
            step = (b * pairs + p) * nq + i
            first_page = (parts - 1 - step % parts) * ppp
            return off + pt[step // parts, first_page + k], 0, 0
        return pl.BlockSpec((1, rows, PAGE_SIZE), index)

    const = lambda rows: pl.BlockSpec((rows, dec), lambda b, p, i, pt: (0, 0), pipeline_mode=pl.Buffered(1))
    out, out_dec = pl.pallas_call(
        functools.partial(_fox_attn_decode_body, tq=tile, tk=tile, hd=hd, ppp=ppp, parts=parts),
        grid_spec=pltpu.PrefetchScalarGridSpec(
            num_scalar_prefetch=1,
            grid=(batch, pairs, nq),
            in_specs=[pl.BlockSpec((1, 2 * hd, tile), lambda b, p, i, pt: (b, p, i)),
                      pl.BlockSpec((1, seq, 2 * LANES), lambda b, p, i, pt: (b, 0, p)),
                      pl.BlockSpec((1, nq, 2 * LANES, tile), lambda b, p, i, pt: (b, 0, p, 0)),
                      pl.BlockSpec((1, F_HEADS, tile), lambda b, p, i, pt: (b, 0, i)),
                      const(d), const(d), const(d), const(F_HEADS)]
                     + [page(d, k) for k in range(ppp)] * 2
                     + [page(F_HEADS, k) for k in range(ppp)],
            out_specs=[pl.BlockSpec((1, tile, 2 * hd), lambda b, p, i, pt: (b, i, p)),
                       pl.BlockSpec((dec, d), lambda b, p, i, pt: (0, 0))],
            scratch_shapes=[pltpu.VMEM((2, 2, tile, tile), F32), pltpu.VMEM((2, 1, tile), F32),
                            pltpu.VMEM((2, LANES, tile), F32),
                            pltpu.VMEM((d, dec), F32), pltpu.VMEM((3, d, 1), F32),
                            pltpu.VMEM((F_HEADS, 1), F32), pltpu.VMEM((F_HEADS, 1), F32),
                            pltpu.VMEM((F_HEADS, 1), F32), pltpu.VMEM((d, PAGE_SIZE), F32)],
        ),
        out_shape=[jax.ShapeDtypeStruct((batch, seq, d), BF16), jax.ShapeDtypeStruct((dec, d), BF16)],
        compiler_params=_params("arbitrary", "arbitrary", "arbitrary"),
        name="fox_attn_decode",
    )(page_table, qt, ka.reshape(batch, seq, slots), vta, crow, qd, kcd, vcd, lfcd,
      *([kp] * ppp), *([vp] * ppp), *([lfp] * ppp))
    return out.reshape(batch * seq, d), out_dec


def kernel(x_prompt, x_sample, state_mlstm_C, state_mlstm_n, state_mlstm_m, cache_fox_k, cache_fox_v, cache_fox_logf, page_table, norm_mix_g, norm_ffn_g, norm_final_g, mlstm_w_in, mlstm_b_i, mlstm_b_f, mlstm_g_hn, mlstm_w_out, fox_w_in, fox_b_f, fox_w_out, ffn_w_up, ffn_w_down):
    batch, seq, d = x_prompt.shape
    dec = x_sample.shape[0]
    depth = norm_mix_g.shape[0]
    hd = d // F_HEADS
    tm = min(ROW_TILE, seq)
    chunk = min(MLSTM_CHUNK, seq)
    xp = x_prompt.reshape(batch * seq, d)
    xs = x_sample.reshape(dec, d)
    g_final = norm_final_g.astype(F32)[None, :]
    c_all = state_mlstm_C.astype(F32)
    n_all = state_mlstm_n.astype(F32)

    mcp, mnp, mmp, mns, mms = [], [], [], [], []
    flp, fks, fvs, fls = [], [], [], []
    c_new = None
    kv32 = None
    for layer in range(depth):
        j = layer // 2
        g_mix = norm_mix_g[layer].astype(F32)[None, :]
        if layer % 2 == 0:
            w = _mlstm_weights(mlstm_w_in[j], mlstm_b_i[j], mlstm_b_f[j])
            ghn = mlstm_g_hn[j].astype(F32)
            qt, k, vt, ot, gc, gr = _mlstm_proj_t(xp, g_mix, w, tm)
            ap, c_p, n_p, m_p = _mlstm_scan(qt, k, vt, ot, gc, gr, ghn, batch, chunk)
            mcp.append(c_p); mnp.append(n_p); mmp.append(m_p[:, :M_HEADS, 0])
            q, k, v, o, gc, _ = _mlstm_proj(xs, g_mix, w, dec)
            as_, c_new, n_s, m_s = _mlstm_step(q, k, v, o, gc, ghn, c_all, n_all, state_mlstm_m[j].astype(F32),
                                               j, min(DEC_SAMPLES, dec), c_new)
            mns.append(n_s); mms.append(m_s)
            w_out = mlstm_w_out[j].astype(BF16)
        else:
            w = _fox_weights(fox_w_in[j], fox_b_f[j])
            qt, ka, k32t, vta, v32t, lft, crow = _fox_proj(xp, g_mix, w, batch, tm, j, depth // 2, kv32)
            kv32 = (k32t, v32t)
            flp.append(lft)
            qd, kd, vd, lfd = _fox_proj_sample(xs, g_mix, w)
            ap, as_ = _fox_attn_decode(qt, ka, vta, crow, tm, qd, kd, vd, lfd,
                                       cache_fox_k, cache_fox_v, cache_fox_logf, page_table, j)
            fks.append(kd.reshape(1, F_HEADS, hd, dec)); fvs.append(vd.reshape(1, F_HEADS, hd, dec))
            fls.append(lfd.reshape(1, F_HEADS, dec))
            w_out = fox_w_out[j].astype(BF16)
        g_ffn = norm_ffn_g[layer].astype(F32)[None, :]
        w_up = ffn_w_up[layer].astype(BF16)
        w_down = ffn_w_down[layer].astype(BF16)
        final = layer == depth - 1
        xp = _mix_ffn(xp, ap, w_out, g_ffn, w_up, w_down, g_final, tm, final, a_transposed=layer % 2 == 0)
        xs = _mix_ffn(xs, as_, w_out, g_ffn, w_up, w_down, g_final, dec, final)

    fkp, fvp = (a.reshape(a.shape[0], batch, F_HEADS, hd, seq) for a in kv32)
    return (xp.reshape(batch, seq, d), xs.reshape(dec, 1, d),
            jnp.stack(mcp), jnp.stack(mnp), jnp.stack(mmp),
            c_new, jnp.stack(mns), jnp.stack(mms),
            jnp.transpose(fkp, (0, 1, 4, 2, 3)), jnp.transpose(fvp, (0, 1, 4, 2, 3)),
            jnp.transpose(jnp.stack(flp), (0, 1, 3, 2)),
            jnp.transpose(jnp.stack(fks), (0, 4, 1, 2, 3)), jnp.transpose(jnp.stack(fvs), (0, 4, 1, 2, 3)),
            jnp.transpose(jnp.stack(fls), (0, 3, 1, 2)))
```

```python
import functools
import math

import jax
import jax.numpy as jnp
from jax import lax
from jax.experimental import pallas as pl
from jax.experimental.pallas import tpu as pltpu

F32 = jnp.float32
BF16 = jnp.bfloat16

EPS = 1e-6
GATE_CAP = 15.0
M_HEADS = 4
F_HEADS = 16
PAGE_SIZE = 128
LOG2E = math.log2(math.e)
LANES = 128
VMEM_LIMIT = 56 * 1024 * 1024

ROW_TILE = 512
MLSTM_CHUNK = 256
DEC_SAMPLES = 8


def _dot(a, b):
    return jnp.dot(a, b, preferred_element_type=F32)


def _dot_nt(a, b):
    return lax.dot_general(a, b, (((1,), (1,)), ((), ())), preferred_element_type=F32)


def _dot_tn(a, b):
    return lax.dot_general(a, b, (((0,), (0,)), ((), ())), preferred_element_type=F32)


def _split3(x):
    hi = x.astype(BF16)
    r = x - hi.astype(F32)
    mid = r.astype(BF16)
    lo = (r - mid.astype(F32)).astype(BF16)
    return hi, mid, lo


def _dot3(a, b, split_lhs):
    if split_lhs:
        hi, mid, lo = _split3(a)
        return _dot(hi, b) + _dot(mid, b) + _dot(lo, b)
    hi, mid, lo = _split3(b)
    return _dot(a, hi) + _dot(a, mid) + _dot(a, lo)


def _rmsnorm(x, g):
    return x * lax.rsqrt(jnp.mean(x * x, axis=-1, keepdims=True) + EPS) * g


def _softcap(z):
    return GATE_CAP * jnp.tanh(z / GATE_CAP)


def _log_sigmoid(x):
    return jnp.minimum(x, 0.0) - jnp.log1p(jnp.exp(-jnp.abs(x)))


def _params(*semantics):
    return pltpu.CompilerParams(dimension_semantics=semantics, vmem_limit_bytes=VMEM_LIMIT)


def _const_spec(shape):
    return pl.BlockSpec(shape, lambda *_: (0,) * len(shape), pipeline_mode=pl.Buffered(1))


def _mlstm_proj_body(x_ref, g_ref, wq_ref, wk_ref, wv_ref, wo_ref, wg_ref, wgt_ref, brow_ref, bcol_ref,
                     q_ref, k_ref, v_ref, o_ref, gc_ref, gr_ref, *, k_scale):
    xn = _rmsnorm(x_ref[...], g_ref[...]).astype(BF16)
    q_ref[...] = _dot(xn, wq_ref[...]).astype(BF16)
    k_ref[...] = (_dot(xn, wk_ref[...]) * k_scale).astype(BF16)
    v_ref[...] = _dot(xn, wv_ref[...]).astype(BF16)
    o_ref[...] = _dot(xn, wo_ref[...])
    zc = _softcap(_dot(xn, wg_ref[...]) + brow_ref[...])
    lane = lax.broadcasted_iota(jnp.int32, zc.shape, 1)
    gc_ref[...] = jnp.where(lane < M_HEADS, zc, _log_sigmoid(zc))
    zr = _softcap(_dot_nt(wgt_ref[...], xn) + bcol_ref[:, 0:1])
    sub = lax.broadcasted_iota(jnp.int32, zr.shape, 0)
    gr_ref[...] = jnp.where(sub < M_HEADS, zr, _log_sigmoid(zr))


def _mlstm_proj(x, g, w, tm):
    n, d = x.shape
    qk, vw = w["wq"].shape[1], w["wv"].shape[1]
    dk = qk // M_HEADS
    row = lambda width: pl.BlockSpec((tm, width), lambda i: (i, 0))
    return pl.pallas_call(
        functools.partial(_mlstm_proj_body, k_scale=dk ** -0.5),
        grid=(n // tm,),
        in_specs=[row(d), _const_spec((1, d)), _const_spec((d, qk)), _const_spec((d, qk)),
                  _const_spec((d, vw)), _const_spec((d, vw)), _const_spec((d, LANES)),
                  _const_spec((8, d)), _const_spec((1, LANES)), _const_spec((8, LANES))],
        out_specs=[row(qk), row(qk), row(vw), row(vw), row(LANES),
                   pl.BlockSpec((8, tm), lambda i: (0, i))],
        out_shape=[jax.ShapeDtypeStruct((n, qk), BF16), jax.ShapeDtypeStruct((n, qk), BF16),
                   jax.ShapeDtypeStruct((n, vw), BF16), jax.ShapeDtypeStruct((n, vw), F32),
                   jax.ShapeDtypeStruct((n, LANES), F32), jax.ShapeDtypeStruct((8, n), F32)],
        compiler_params=_params("parallel"),
        name="mlstm_proj",
    )(x, g, w["wq"], w["wk"], w["wv"], w["wo"], w["wg"], w["wgt"], w["brow"], w["bcol"])


def _mlstm_proj_t_body(x_ref, g_ref, wqt_ref, wkt_ref, wvt_ref, wot_ref, wg_ref, wgt_ref, brow_ref, bcol_ref,
                       qt_ref, k_ref, vt_ref, ot_ref, gc_ref, gr_ref, *, k_scale):
    xn = _rmsnorm(x_ref[...], g_ref[...]).astype(BF16)
    qt_ref[...] = _dot_nt(wqt_ref[...], xn).astype(BF16)
    k_ref[...] = (_dot_nt(xn, wkt_ref[...]) * k_scale).astype(BF16)
    vt_ref[...] = _dot_nt(wvt_ref[...], xn).astype(BF16)
    ot_ref[...] = _dot_nt(wot_ref[...], xn)
    zc = _softcap(_dot(xn, wg_ref[...]) + brow_ref[...])
    lane = lax.broadcasted_iota(jnp.int32, zc.shape, 1)
    gc_ref[...] = jnp.where(lane < M_HEADS, zc, _log_sigmoid(zc))
    zr = _softcap(_dot_nt(wgt_ref[...], xn) + bcol_ref[:, 0:1])
    sub = lax.broadcasted_iota(jnp.int32, zr.shape, 0)
    gr_ref[...] = jnp.where(sub < M_HEADS, zr, _log_sigmoid(zr))


def _mlstm_proj_t(x, g, w, tm):
    n, d = x.shape
    qk, vw = w["wqt"].shape[0], w["wvt"].shape[0]
    dk = qk // M_HEADS
    row = lambda width: pl.BlockSpec((tm, width), lambda i: (i, 0))
    tspec = lambda rows: pl.BlockSpec((rows, tm), lambda i: (0, i))
    return pl.pallas_call(
        functools.partial(_mlstm_proj_t_body, k_scale=dk ** -0.5),
        grid=(n // tm,),
        in_specs=[row(d), _const_spec((1, d)), _const_spec((qk, d)), _const_spec((qk, d)),
                  _const_spec((vw, d)), _const_spec((vw, d)), _const_spec((d, LANES)),
                  _const_spec((8, d)), _const_spec((1, LANES)), _const_spec((8, LANES))],
        out_specs=[tspec(qk), row(qk), tspec(vw), tspec(vw), row(LANES), tspec(8)],
        out_shape=[jax.ShapeDtypeStruct((qk, n), BF16), jax.ShapeDtypeStruct((n, qk), BF16),
                   jax.ShapeDtypeStruct((vw, n), BF16), jax.ShapeDtypeStruct((vw, n), F32),
                   jax.ShapeDtypeStruct((n, LANES), F32), jax.ShapeDtypeStruct((8, n), F32)],
        compiler_params=_params("parallel"),
        name="mlstm_proj_t",
    )(x, g, w["wqt"], w["wkt"], w["wvt"], w["wot"], w["wg"], w["wgt"], w["brow"], w["bcol"])


def _mlstm_weights(w_in, b_i, b_f):
    d = w_in.shape[0]
    h = M_HEADS
    vw = d
    qk = (w_in.shape[1] - 2 * vw - 2 * h) // 2
    wb = w_in.astype(BF16)
    wt = w_in.T.astype(BF16)
    gates = wb[:, 2 * qk + 2 * vw:]
    bias = jnp.concatenate([b_i, b_f]).astype(F32)
    return {
        "wq": wb[:, :qk], "wk": wb[:, qk:2 * qk], "wv": wb[:, 2 * qk:2 * qk + vw],
        "wo": wb[:, 2 * qk + vw:2 * qk + 2 * vw],
        "wqt": wt[:qk], "wkt": wt[qk:2 * qk], "wvt": wt[2 * qk:2 * qk + vw],
        "wot": wt[2 * qk + vw:2 * qk + 2 * vw],
        "wg": jnp.pad(gates, ((0, 0), (0, LANES - 2 * h))),
        "wgt": wt[2 * qk + 2 * vw:],
        "brow": jnp.pad(bias, (0, LANES - 2 * h))[None, :],
        "bcol": jnp.broadcast_to(bias[:, None], (2 * h, LANES)),
    }


def _mlstm_scan_body(qt_ref, k_ref, vt_ref, ot_ref, gc_ref, gr_ref, ghn_ref,
                     out_ref, cfin_ref, nfin_ref, mfin_ref, c_s, n_s, m_s, *, t, dk, dv):
    c_idx = pl.program_id(1)

    @pl.when(c_idx == 0)
    def _():
        c_s[...] = jnp.zeros_like(c_s)
        n_s[...] = jnp.zeros_like(n_s)
        m_s[...] = jnp.zeros_like(m_s)

    row = lax.broadcasted_iota(jnp.int32, (t, t), 0)
    col = lax.broadcasted_iota(jnp.int32, (t, t), 1)
    src_le_tgt = row <= col
    tri_l = (col <= row).astype(BF16)
    tri_u = src_le_tgt.astype(BF16)
    gc = gc_ref[...]
    gr = gr_ref[...]
    bcol_all = _dot3(tri_l, gc, split_lhs=False)
    brow_all = _dot3(gr, tri_u, split_lhs=True)

    for h in range(M_HEADS):
        b_col = bcol_all[:, M_HEADS + h:M_HEADS + h + 1]
        b_row = brow_all[M_HEADS + h:M_HEADS + h + 1, :]
        u_col = gc[:, h:h + 1] - b_col
        u_row = gr[h:h + 1, :] - b_row
        m_prev = m_s[h:h + 1, 0:1]
        qth = qt_ref[h * dk:(h + 1) * dk, :]
        kh = k_ref[:, h * dk:(h + 1) * dk]
        vth = vt_ref[h * dv:(h + 1) * dv, :]
        ch = c_s[h]
        nh = n_s[h:h + 1, :]

        dmat = jnp.where(src_le_tgt, u_col + b_row, -jnp.inf)
        inter = b_row + m_prev
        mt = jnp.maximum(jnp.max(dmat, axis=0, keepdims=True), inter)
        a = jnp.exp(inter - mt)
        s = _dot(kh, qth) * jnp.exp(dmat - mt)
        num = _dot(vth, s.astype(BF16)) + a * _dot(ch.astype(BF16), qth)
        nq = _dot(jnp.broadcast_to(nh, (8, dk)).astype(BF16), qth)[0:1]
        den = jnp.sum(s, axis=0, keepdims=True) + a * nq
        hh = num / jnp.maximum(jnp.abs(den), jnp.exp(-mt))
        hn = hh * lax.rsqrt(jnp.mean(hh * hh, axis=0, keepdims=True) + EPS) * ghn_ref[:, h:h + 1]
        gated = jax.nn.sigmoid(ot_ref[h * dv:(h + 1) * dv, :]) * hn
        out_ref[h * dv:(h + 1) * dv, :] = gated.astype(BF16)

        bl = b_col[t - 1:t, :]
        g_row = bl + u_row
        m_new = jnp.maximum(bl + m_prev, jnp.max(g_row, axis=1, keepdims=True))
        wg_row = jnp.exp(g_row - m_new)
        ac = jnp.exp(bl + m_prev - m_new)
        vw = (vth.astype(F32) * wg_row).astype(BF16)
        c_s[h] = ac * ch + _dot(vw, kh)
        n_s[h:h + 1, :] = ac * nh + _dot(wg_row.astype(BF16), kh)
        m_s[h:h + 1, :] = jnp.broadcast_to(m_new, (1, LANES))

    @pl.when(c_idx == pl.num_programs(1) - 1)
    def _():
        cfin_ref[0] = c_s[...]
        nfin_ref[0] = n_s[0:M_HEADS, :]
        mfin_ref[0] = m_s[...]


def _mlstm_scan(qt, k, vt, ot, gc, gr, ghn, batch, t):
    n, qk = k.shape
    vw = vt.shape[0]
    dk, dv = qk // M_HEADS, vw // M_HEADS
    nc = n // batch // t
    row = lambda width: pl.BlockSpec((t, width), lambda b, c: (b * nc + c, 0))
    tspec = lambda rows: pl.BlockSpec((rows, t), lambda b, c: (0, b * nc + c))
    ghn_cols = jnp.pad(ghn.T, ((0, 0), (0, LANES - M_HEADS)))
    return pl.pallas_call(
        functools.partial(_mlstm_scan_body, t=t, dk=dk, dv=dv),
        grid=(batch, nc),
        in_specs=[tspec(qk), row(qk), tspec(vw), tspec(vw), row(LANES), tspec(8),
                  _const_spec((dv, LANES))],
        out_specs=[tspec(vw),
                   pl.BlockSpec((1, M_HEADS, dv, dk), lambda b, c: (b, 0, 0, 0)),
                   pl.BlockSpec((1, M_HEADS, dk), lambda b, c: (b, 0, 0)),
                   pl.BlockSpec((1, 8, LANES), lambda b, c: (b, 0, 0))],
        out_shape=[jax.ShapeDtypeStruct((vw, n), BF16),
                   jax.ShapeDtypeStruct((batch, M_HEADS, dv, dk), F32),
                   jax.ShapeDtypeStruct((batch, M_HEADS, dk), F32),
                   jax.ShapeDtypeStruct((batch, 8, LANES), F32)],
        scratch_shapes=[pltpu.VMEM((M_HEADS, dv, dk), F32), pltpu.VMEM((8, dk), F32),
                        pltpu.VMEM((8, LANES), F32)],
        compiler_params=_params("arbitrary", "arbitrary"),
        name="mlstm_scan",
    )(qt, k, vt, ot, gc, gr, ghn_cols)


def _mlstm_step_body(q_ref, k_ref, v_ref, o_ref, gc_ref, ghn_ref, c_ref, n_ref, m_ref,
                     out_ref, cn_ref, nn_ref, mn_ref, *, sb, dk, dv):
    gates = gc_ref[...]
    m_all = m_ref[...]
    q_all = q_ref[...].astype(F32)
    k_all = k_ref[...].astype(F32)
    v_all = v_ref[...].astype(F32)
    sig_o = jax.nn.sigmoid(o_ref[...])
    sub = lax.broadcasted_iota(jnp.int32, (8, dv), 0)
    rows = []
    for s in range(sb):
        outs = []
        for h in range(M_HEADS):
            li = gates[s:s + 1, h:h + 1]
            lf = gates[s:s + 1, M_HEADS + h:M_HEADS + h + 1]
            m_prev = m_all[s:s + 1, h:h + 1]
            qf = q_all[s:s + 1, h * dk:(h + 1) * dk]
            kf = k_all[s:s + 1, h * dk:(h + 1) * dk]
            vf = v_all[s:s + 1, h * dv:(h + 1) * dv]
            ch = c_ref[s, h]
            nh = n_ref[s, h:h + 1, :]

            inter = lf + m_prev
            mt = jnp.maximum(li, inter)
            w = jnp.exp(li - mt)
            a = jnp.exp(inter - mt)
            sc = jnp.sum(qf * kf, axis=1, keepdims=True) * w
            cq = _dot_nt(jnp.broadcast_to(qf, (8, dk)).astype(BF16), ch.astype(BF16))[0:1]
            num = sc * vf + a * cq
            den = sc + a * jnp.sum(nh * qf, axis=1, keepdims=True)
            hh = num / jnp.maximum(jnp.abs(den), jnp.exp(-mt))
            hn = hh * lax.rsqrt(jnp.mean(hh * hh, axis=1, keepdims=True) + EPS) * ghn_ref[h:h + 1, :]
            outs.append(sig_o[s:s + 1, h * dv:(h + 1) * dv] * hn)

            vw8 = jnp.where(sub == 0, jnp.broadcast_to(vf * w, (8, dv)), 0.0).astype(BF16)
            k8 = jnp.broadcast_to(kf, (8, dk)).astype(BF16)
            cn_ref[s, h] = a * ch + _dot_tn(vw8, k8)
            nn_ref[s, h:h + 1, :] = a * nh + w * kf
            mn_ref[s:s + 1, h:h + 1] = mt
        rows.append(jnp.concatenate(outs, axis=1))
    out_ref[...] = jnp.concatenate(rows, axis=0).astype(BF16)


N_MLSTM_STEP_IN = 9


def _mlstm_step(q, k, v, o, gc, ghn, c_all, n_all, m0, layer, sb, c_prev):
    n, qk = q.shape
    vw = v.shape[1]
    dk, dv = qk // M_HEADS, vw // M_HEADS
    flat = lambda a: a.reshape((-1,) + a.shape[2:])
    off = layer * (n // sb)
    row = lambda width: pl.BlockSpec((sb, width), lambda i: (i, 0))
    cspec = pl.BlockSpec((sb, M_HEADS, dv, dk), lambda i: (off + i, 0, 0, 0))
    nshape = (sb, M_HEADS, dk)
    body = functools.partial(_mlstm_step_body, sb=sb, dk=dk, dv=dv)
    operands = [q, k, v, o, gc, ghn, flat(c_all), flat(n_all), m0]
    in_specs = [row(qk), row(qk), row(vw), row(vw), row(LANES), _const_spec((M_HEADS, dv)),
                cspec, pl.BlockSpec(nshape, lambda i: (off + i, 0, 0)), row(M_HEADS)]
    aliases = {}
    if c_prev is not None:
        operands.append(flat(c_prev))
        in_specs.append(pl.BlockSpec(memory_space=pl.ANY))
        aliases = {N_MLSTM_STEP_IN: 1}
        inner = body

        def body(*refs):
            inner(*refs[:N_MLSTM_STEP_IN], *refs[N_MLSTM_STEP_IN + 1:])
    out, c_new, n_new, m_new = pl.pallas_call(
        body,
        grid=(n // sb,),
        in_specs=in_specs,
        out_specs=[row(vw), cspec, pl.BlockSpec(nshape, lambda i: (i, 0, 0)), row(M_HEADS)],
        out_shape=[jax.ShapeDtypeStruct((n, vw), BF16),
                   jax.ShapeDtypeStruct(flat(c_all).shape, F32), jax.ShapeDtypeStruct(n_all.shape[1:], F32),
                   jax.ShapeDtypeStruct(m0.shape, F32)],
        input_output_aliases=aliases,
        compiler_params=_params("parallel"),
        name="mlstm_step",
    )(*operands)
    return out, c_new.reshape(c_all.shape), n_new, m_new


def _mix_ffn_body(x_ref, a_ref, wout_ref, g_ref, wup_ref, wdn_ref, gfin_ref, o_ref, *, n_chunks, final, a_transposed):
    mixed = _dot_tn(a_ref[...], wout_ref[...]) if a_transposed else _dot(a_ref[...], wout_ref[...])
    x1 = x_ref[...] + mixed
    xn = _rmsnorm(x1, g_ref[...]).astype(BF16)
    fc = wup_ref.shape[1] // n_chunks
    acc = x1
    for c in range(n_chunks):
        hid = jnp.maximum(_dot(xn, wup_ref[:, c * fc:(c + 1) * fc]), 0.0)
        acc = acc + _dot((hid * hid).astype(BF16), wdn_ref[c * fc:(c + 1) * fc, :])
    o_ref[...] = _rmsnorm(acc, gfin_ref[...]) if final else acc


def _mix_ffn(x, a, w_out, g, w_up, w_down, g_final, tm, final, a_transposed=False):
    n, d = x.shape
    dff = w_up.shape[1]
    row = lambda width: pl.BlockSpec((tm, width), lambda i: (i, 0))
    a_spec = pl.BlockSpec((d, tm), lambda i: (0, i)) if a_transposed else row(d)
    return pl.pallas_call(
        functools.partial(_mix_ffn_body, n_chunks=4, final=final, a_transposed=a_transposed),
        grid=(n // tm,),
        in_specs=[row(d), a_spec, _const_spec((d, d)), _const_spec((1, d)),
                  _const_spec((d, dff)), _const_spec((dff, d)), _const_spec((1, d))],
        out_specs=row(d),
        out_shape=jax.ShapeDtypeStruct((n, d), F32),
        compiler_params=_params("parallel"),
        name="mix_ffn",
    )(x, a, w_out, g, w_up, w_down, g_final)


N_SPLIT = 3


def _bias_placement(hd):
    rows = jnp.arange(N_SPLIT * LANES)
    piece, h = rows // LANES, rows % LANES
    target = jnp.where(h < F_HEADS, h * LANES + hd + piece, -1)
    return -(target[:, None] == jnp.arange(F_HEADS * LANES)[None, :]).astype(BF16)


def _fox_weights(w_in, b_f):
    d = w_in.shape[0]
    hd = d // F_HEADS
    wt = w_in.T.astype(BF16)
    wkt, wft = wt[d:2 * d], wt[3 * d:]
    bias = b_f.astype(F32)
    wkt_slots = jnp.pad(wkt.reshape(F_HEADS, hd, d), ((0, 0), (0, LANES - hd), (0, 0))).reshape(F_HEADS * LANES, d)
    return {
        "wqt": wt[:d], "wkt": wkt, "wkt_slots": wkt_slots, "wvt": wt[2 * d:3 * d], "wft": wft,
        "wf": jnp.pad(wft.T, ((0, 0), (0, LANES - F_HEADS))),
        "bfrow": jnp.pad(bias, (0, LANES - F_HEADS))[None, :],
        "bfcol": jnp.broadcast_to(bias[:, None], (F_HEADS, LANES)),
        "place": _bias_placement(hd),
    }


def _fox_proj_body(x_ref, g_ref, wqt_ref, wkt_ref, wks_ref, wvt_ref, wf_ref, wft_ref, bfrow_ref, bfcol_ref, place_ref,
                   qt_ref, ka_ref, k32t_ref, vta_ref, v32t_ref, lft_ref, crow_ref,
                   carry_row_s, carry_col_s, *, q_scale, hd):
    @pl.when(pl.program_id(1) == 0)
    def _():
        carry_row_s[...] = jnp.zeros_like(carry_row_s)
        carry_col_s[...] = jnp.zeros_like(carry_col_s)

    xn = _rmsnorm(x_ref[...], g_ref[...]).astype(BF16)
    tm, d = xn.shape
    qt_ref[0] = (_dot_nt(wqt_ref[...], xn) * q_scale).astype(BF16)
    k32t_ref[0, 0] = _dot_nt(wkt_ref[...], xn)
    vt = _dot_nt(wvt_ref[...], xn)
    v32t_ref[0, 0] = vt
    sub = lax.broadcasted_iota(jnp.int32, (F_HEADS, LANES - hd, tm), 1)
    ones_row = jnp.where(sub == 0, 1.0, 0.0)
    vta = jnp.concatenate([vt.reshape(F_HEADS, hd, tm), ones_row], axis=1)
    vta_ref[0, 0] = vta.reshape(F_HEADS * LANES, tm).astype(BF16)

    zc = _dot(xn, wf_ref[...]) + bfrow_ref[...]
    lane = lax.broadcasted_iota(jnp.int32, zc.shape, 1)
    lfc = jnp.where(lane < F_HEADS, _log_sigmoid(zc), 0.0)
    lfr = _log_sigmoid(_dot_nt(wft_ref[...], xn) + bfcol_ref[:, 0:1])
    lft_ref[0] = lfr
    row = lax.broadcasted_iota(jnp.int32, (tm, tm), 0)
    col = lax.broadcasted_iota(jnp.int32, (tm, tm), 1)
    ccol = _dot3((col <= row).astype(BF16), lfc, split_lhs=False) + carry_row_s[...]
    carry_row_s[...] = ccol[tm - 1:tm, :]
    crow = _dot3(lfr, (row <= col).astype(BF16), split_lhs=True) + carry_col_s[:, 0:1]
    carry_col_s[...] = jnp.broadcast_to(crow[:, tm - 1:tm], carry_col_s.shape)
    crow_ref[0] = crow * LOG2E
    pieces = jnp.concatenate(_split3(ccol * LOG2E), axis=1)
    ka_ref[...] = (_dot_nt(xn, wks_ref[...]) + _dot(pieces, place_ref[...])).astype(BF16)


N_FOX_PROJ_IN = 11


def _fox_proj(x, g, w, batch, tm, slab, n_slabs, kv_prev):
    n, d = x.shape
    hd = d // F_HEADS
    seq = n // batch
    ns = seq // tm
    slots = F_HEADS * LANES
    row = lambda width: pl.BlockSpec((tm, width), lambda b, i: (b * ns + i, 0))
    tspec = lambda rows: pl.BlockSpec((1, rows, tm), lambda b, i: (b, 0, i))
    slab_spec = pl.BlockSpec((1, 1, d, tm), lambda b, i: (slab, b, 0, i))
    slab_shape = jax.ShapeDtypeStruct((n_slabs, batch, d, seq), F32)
    body = functools.partial(_fox_proj_body, q_scale=hd ** -0.5 * LOG2E, hd=hd)
    operands = [x, g, w["wqt"], w["wkt"], w["wkt_slots"], w["wvt"], w["wf"], w["wft"], w["bfrow"], w["bfcol"], w["place"]]
    in_specs = [row(d), _const_spec((1, d)), _const_spec((d, d)), _const_spec((d, d)), _const_spec((slots, d)),
                _const_spec((d, d)), _const_spec((d, LANES)), _const_spec((F_HEADS, d)), _const_spec((1, LANES)),
                _const_spec((F_HEADS, LANES)), _const_spec((N_SPLIT * LANES, slots))]
    aliases = {}
    if kv_prev is not None:
        operands += list(kv_prev)
        in_specs += [pl.BlockSpec(memory_space=pl.ANY)] * 2
        aliases = {N_FOX_PROJ_IN: 2, N_FOX_PROJ_IN + 1: 4}
        inner = body

        def body(*refs):
            inner(*refs[:N_FOX_PROJ_IN], *refs[N_FOX_PROJ_IN + 2:])
    return pl.pallas_call(
        body,
        grid=(batch, ns),
        in_specs=in_specs,
        out_specs=[tspec(d), row(slots), slab_spec,
                   pl.BlockSpec((1, 1, slots, tm), lambda b, i: (b, i, 0, 0)), slab_spec,
                   tspec(F_HEADS), tspec(F_HEADS)],
        out_shape=[jax.ShapeDtypeStruct((batch, d, seq), BF16), jax.ShapeDtypeStruct((n, slots), BF16), slab_shape,
                   jax.ShapeDtypeStruct((batch, ns, slots, tm), BF16), slab_shape,
                   jax.ShapeDtypeStruct((batch, F_HEADS, seq), F32), jax.ShapeDtypeStruct((batch, F_HEADS, seq), F32)],
        scratch_shapes=[pltpu.VMEM((1, LANES), F32), pltpu.VMEM((F_HEADS, LANES), F32)],
        input_output_aliases=aliases,
        compiler_params=_params("arbitrary", "arbitrary"),
        name="fox_proj",
    )(*operands)


def _attention_tile(qt_ref, ka_ref, vta_ref, crow_ref, o_ref, s_s, m_s, acc_s, *, tq, tk, hd, overlapped_work):
    pair = pl.program_id(1)
    qi = pl.program_id(2)
    sub = lax.broadcasted_iota(jnp.int32, (LANES - hd, tq), 0)
    ones_rows = jnp.where(sub < N_SPLIT, 1.0, 0.0).astype(BF16)
    qa = [jnp.concatenate([qt_ref[0, h * hd:(h + 1) * hd, :], ones_rows], axis=0) for h in range(2)]
    cq = [crow_ref[0, pl.ds(2 * pair + h, 1), :] for h in range(2)]

    def scores(j, slot, heads=(0, 1)):
        start = pl.multiple_of(j * tk, tk)
        for h in heads:
            s_s[slot, h] = _dot(ka_ref[0, pl.ds(start, tk), h * LANES:(h + 1) * LANES], qa[h])

    def update(j, slot, masked, heads=(0, 1)):
        for h in heads:
            t = s_s[slot, h]
            if masked:
                krow = lax.broadcasted_iota(jnp.int32, (tk, tq), 0)
                qcol = lax.broadcasted_iota(jnp.int32, (tk, tq), 1)
                t = jnp.where(krow <= qcol, t, -jnp.inf)
            m = m_s[h]
            m_new = jnp.maximum(m, jnp.max(t, axis=0, keepdims=True) + cq[h])
            p = jnp.exp2(t + (cq[h] - m_new))
            pv = _dot(vta_ref[0, j, h * LANES:(h + 1) * LANES, :], p.astype(BF16))
            acc_s[h] = jnp.exp2(m - m_new) * acc_s[h] + pv
            m_s[h] = m_new

    def finish():
        ot = jnp.concatenate([acc_s[h, :hd, :] / acc_s[h, hd:hd + 1, :] for h in range(2)], axis=0)
        o_ref[0] = ot.T.astype(BF16)

    m_s[...] = jnp.full(m_s.shape, -jnp.inf, F32)
    acc_s[...] = jnp.zeros_like(acc_s)
    scores(0, 0)
    overlapped_work()

    def two_blocks(j):
        for h in range(2):
            scores(j + 1, 1, (h,))
            update(j, 0, False, (h,))
        for h in range(2):
            scores(j + 2, 0, (h,))
            update(j + 1, 1, False, (h,))

    def body4(i, _):
        two_blocks(4 * i)
        two_blocks(4 * i + 2)
        return 0

    def body2(i, _):
        two_blocks(2 * i)
        return 0

    lax.fori_loop(0, qi // 4, body4, 0)
    lax.fori_loop(2 * (qi // 4), qi // 2, body2, 0)

    @pl.when(qi % 2 == 0)
    def _():
        update(qi, 0, True)
        finish()

    @pl.when(qi % 2 == 1)
    def _():
        for h in range(2):
            scores(qi, 1, (h,))
            update(qi - 1, 0, False, (h,))
        update(qi, 1, True)
        finish()


def _fox_proj_sample_body(x_ref, g_ref, wqt_ref, wkt_ref, wvt_ref, wft_ref, bfcol_ref,
                          qt_ref, kt_ref, vt_ref, lft_ref, qr_ref, *, q_scale):
    xn = _rmsnorm(x_ref[...], g_ref[...]).astype(BF16)
    qr_ref[...] = _dot_nt(xn, wqt_ref[...]) * q_scale
    qt_ref[...] = _dot_nt(wqt_ref[...], xn) * q_scale
    kt_ref[...] = _dot_nt(wkt_ref[...], xn)
    vt_ref[...] = _dot_nt(wvt_ref[...], xn)
    lft_ref[...] = _log_sigmoid(_dot_nt(wft_ref[...], xn) + bfcol_ref[:, 0:1])


def _fox_proj_sample(x, g, w):
    n, d = x.shape
    hd = d // F_HEADS
    return pl.pallas_call(
        functools.partial(_fox_proj_sample_body, q_scale=hd ** -0.5),
        grid=(1,),
        in_specs=[_const_spec((n, d)), _const_spec((1, d)), _const_spec((d, d)), _const_spec((d, d)),
                  _const_spec((d, d)), _const_spec((F_HEADS, d)), _const_spec((F_HEADS, LANES))],
        out_specs=[pl.BlockSpec((d, n), lambda i: (0, 0))] * 3 + [pl.BlockSpec((F_HEADS, n), lambda i: (0, 0)),
                                                                   pl.BlockSpec((n, d), lambda i: (0, 0))],
        out_shape=[jax.ShapeDtypeStruct((d, n), F32)] * 3 + [jax.ShapeDtypeStruct((F_HEADS, n), F32),
                                                              jax.ShapeDtypeStruct((n, d), F32)],
        compiler_params=_params("arbitrary"),
        name="fox_proj_sample",
    )(x, g, w["wqt"], w["wkt"], w["wvt"], w["wft"], w["bfcol"])


def _decode_part(step, qt_ref, kct_ref, vct_ref, lfct_ref, qr_ref, k_refs, v_refs, lf_refs, od_ref,
                 ot_s, col_s, dm_s, dl_s, dsuf_s, dacc_s, *, hd, parts):
    d, nb = qt_ref.shape
    heads = d // hd
    sample = step // parts
    part = step % parts
    pick = lax.broadcasted_iota(jnp.int32, (d, nb), 1) == sample

    def head_sum(x):
        return jnp.sum(x.reshape(heads, hd, x.shape[1]), axis=1)

    def head_rows(x):
        return jnp.broadcast_to(x[:, None, :], (heads, hd, x.shape[1])).reshape(d, x.shape[1])

    def start():
        @pl.when(step == 0)
        def _():
            ot_s[...] = jnp.zeros_like(ot_s)

        @pl.when(part == 0)
        def _():
            for c, ref in enumerate((qt_ref, kct_ref, vct_ref)):
                col_s[c] = jnp.sum(jnp.where(pick, ref[...], 0.0), axis=1, keepdims=True)
            pick_h = lax.broadcasted_iota(jnp.int32, (heads, nb), 1) == sample
            dsuf_s[...] = jnp.sum(jnp.where(pick_h, lfct_ref[...], 0.0), axis=1, keepdims=True)
            dm_s[...] = head_sum(col_s[0] * col_s[1])
            dl_s[...] = jnp.ones_like(dl_s)
            dacc_s[...] = jnp.zeros_like(dacc_s)

    def main():
        sub = lax.broadcasted_iota(jnp.int32, (heads, d), 0)
        lan = lax.broadcasted_iota(jnp.int32, (heads, d), 1)
        qrow = jnp.broadcast_to(qr_ref[pl.ds(sample, 1), :], (heads, d))
        qblk = jnp.where(lan // hd == sub, qrow, 0.0).astype(BF16)
        ri = lax.broadcasted_iota(jnp.int32, (PAGE_SIZE, PAGE_SIZE), 0)
        ci = lax.broadcasted_iota(jnp.int32, (PAGE_SIZE, PAGE_SIZE), 1)
        later = (ri > ci).astype(BF16)
        suffix = dsuf_s[...]
        logits = [None] * len(k_refs)
        for pg in reversed(range(len(k_refs))):
            lf = lf_refs[pg][0]
            bias = suffix + _dot3(lf, later, split_lhs=True)
            suffix = suffix + jnp.sum(lf, axis=1, keepdims=True)
            logits[pg] = _dot(qblk, k_refs[pg][0].astype(BF16)) + bias
        dsuf_s[...] = suffix
        m_old = dm_s[...]
        m = m_old
        for lg in logits:
            m = jnp.maximum(m, jnp.max(lg, axis=1, keepdims=True))
        alpha = jnp.exp(m_old - m)
        l = alpha * dl_s[...]
        acc = head_rows(alpha) * dacc_s[...]
        for pg, lg in enumerate(logits):
            p = jnp.exp(lg - m)
            l = l + jnp.sum(p, axis=1, keepdims=True)
            acc = acc + v_refs[pg][0] * head_rows(p)
        dm_s[...] = m
        dl_s[...] = l
        dacc_s[...] = acc

    def finish():
        @pl.when(part == parts - 1)
        def _():
            p_cur = jnp.exp(head_sum(col_s[0] * col_s[1]) - dm_s[...])
            out = ((jnp.sum(dacc_s[...], axis=1, keepdims=True) + col_s[2] * head_rows(p_cur))
                   / head_rows(dl_s[...]))
            ot_s[...] = jnp.where(pick, out, ot_s[...])

        @pl.when(step == nb * parts - 1)
        def _():
            od_ref[...] = ot_s[...].T.astype(BF16)

    return start, main, finish


def _fox_attn_decode_body(pt_ref, qt_ref, ka_ref, vta_ref, crow_ref, qd_ref, kcd_ref, vcd_ref, lfcd_ref, qr_ref, *rest,
                          tq, tk, hd, ppp, parts):
    del pt_ref
    k_refs, v_refs, lf_refs = rest[:ppp], rest[ppp:2 * ppp], rest[2 * ppp:3 * ppp]
    o_ref, od_ref, s_s, m_s, acc_s, ot_s, col_s, dm_s, dl_s, dsuf_s, dacc_s = rest[3 * ppp:]
    step = (pl.program_id(0) * pl.num_programs(1) + pl.program_id(1)) * pl.num_programs(2) + pl.program_id(2)
    start, main, finish = _decode_part(step, qd_ref, kcd_ref, vcd_ref, lfcd_ref, qr_ref, k_refs, v_refs, lf_refs, od_ref,
                                       ot_s, col_s, dm_s, dl_s, dsuf_s, dacc_s, hd=hd, parts=parts)
    start()
    _attention_tile(qt_ref, ka_ref, vta_ref, crow_ref, o_ref, s_s, m_s, acc_s, tq=tq, tk=tk, hd=hd,
                    overlapped_work=main)
    finish()


def _fox_attn_decode(qt, ka, vta, crow, tile, qd, kcd, vcd, lfcd, qrd, k_cache, v_cache, lf_cache, page_table, layer):
    batch, d, seq = qt.shape
    hd = d // F_HEADS
    nq = seq // tile
    pairs = F_HEADS // 2
    slots = F_HEADS * LANES
    dec = qd.shape[1]
    n_pages = page_table.shape[1]
    n_pool = k_cache.shape[1]
    steps = batch * pairs * nq
    parts = steps // dec
    assert parts * dec == steps and n_pages % parts == 0, (steps, dec, n_pages)
    ppp = n_pages // parts
    kp = jnp.transpose(k_cache, (0, 1, 3, 4, 2)).reshape(-1, d, PAGE_SIZE)
    vp = jnp.transpose(v_cache, (0, 1, 3, 4, 2)).reshape(-1, d, PAGE_SIZE)
    lfp = jnp.transpose(lf_cache, (0, 1, 3, 2)).reshape(-1, F_HEADS, PAGE_SIZE)
    off = layer * n_pool

    def page(rows, k):
        def index(b, p, i, pt):
            step = (b * pairs + p) * nq + i
            first_page = (parts - 1 - step % parts) * ppp
            return off + pt[step // parts, first_page + k], 0, 0
        return pl.BlockSpec((1, rows, PAGE_SIZE), index)

    const = lambda rows: pl.BlockSpec((rows, dec), lambda b, p, i, pt: (0, 0), pipeline_mode=pl.Buffered(1))
    out, out_dec = pl.pallas_call(
        functools.partial(_fox_attn_decode_body, tq=tile, tk=tile, hd=hd, ppp=ppp, parts=parts),
        grid_spec=pltpu.PrefetchScalarGridSpec(
            num_scalar_prefetch=1,
            grid=(batch, pairs, nq),
            in_specs=[pl.BlockSpec((1, 2 * hd, tile), lambda b, p, i, pt: (b, p, i)),
                      pl.BlockSpec((1, seq, 2 * LANES), lambda b, p, i, pt: (b, 0, p)),
                      pl.BlockSpec((1, nq, 2 * LANES, tile), lambda b, p, i, pt: (b, 0, p, 0)),
                      pl.BlockSpec((1, F_HEADS, tile), lambda b, p, i, pt: (b, 0, i)),
                      const(d), const(d), const(d), const(F_HEADS),
                      pl.BlockSpec((dec, d), lambda b, p, i, pt: (0, 0), pipeline_mode=pl.Buffered(1))]
                     + [page(d, k) for k in range(ppp)] * 2
                     + [page(F_HEADS, k) for k in range(ppp)],
            out_specs=[pl.BlockSpec((1, tile, 2 * hd), lambda b, p, i, pt: (b, i, p)),
                       pl.BlockSpec((dec, d), lambda b, p, i, pt: (0, 0))],
            scratch_shapes=[pltpu.VMEM((2, 2, tile, tile), F32), pltpu.VMEM((2, 1, tile), F32),
                            pltpu.VMEM((2, LANES, tile), F32),
                            pltpu.VMEM((d, dec), F32), pltpu.VMEM((3, d, 1), F32),
                            pltpu.VMEM((F_HEADS, 1), F32), pltpu.VMEM((F_HEADS, 1), F32),
                            pltpu.VMEM((F_HEADS, 1), F32), pltpu.VMEM((d, PAGE_SIZE), F32)],
        ),
        out_shape=[jax.ShapeDtypeStruct((batch, seq, d), BF16), jax.ShapeDtypeStruct((dec, d), BF16)],
        compiler_params=_params("arbitrary", "arbitrary", "arbitrary"),
        name="fox_attn_decode",
    )(page_table, qt, ka.reshape(batch, seq, slots), vta, crow, qd, kcd, vcd, lfcd, qrd,
      *([kp] * ppp), *([vp] * ppp), *([lfp] * ppp))
    return out.reshape(batch * seq, d), out_dec


def kernel(x_prompt, x_sample, state_mlstm_C, state_mlstm_n, state_mlstm_m, cache_fox_k, cache_fox_v, cache_fox_logf, page_table, norm_mix_g, norm_ffn_g, norm_final_g, mlstm_w_in, mlstm_b_i, mlstm_b_f, mlstm_g_hn, mlstm_w_out, fox_w_in, fox_b_f, fox_w_out, ffn_w_up, ffn_w_down):
    batch, seq, d = x_prompt.shape
    dec = x_sample.shape[0]
    depth = norm_mix_g.shape[0]
    hd = d // F_HEADS
    tm = min(ROW_TILE, seq)
    chunk = min(MLSTM_CHUNK, seq)
    xp = x_prompt.reshape(batch * seq, d)
    xs = x_sample.reshape(dec, d)
    g_final = norm_final_g.astype(F32)[None, :]
    c_all = state_mlstm_C.astype(F32)
    n_all = state_mlstm_n.astype(F32)

    mcp, mnp, mmp, mns, mms = [], [], [], [], []
    flp, fks, fvs, fls = [], [], [], []
    c_new = None
    kv32 = None
    for layer in range(depth):
        j = layer // 2
        g_mix = norm_mix_g[layer].astype(F32)[None, :]
        if layer % 2 == 0:
            w = _mlstm_weights(mlstm_w_in[j], mlstm_b_i[j], mlstm_b_f[j])
            ghn = mlstm_g_hn[j].astype(F32)
            qt, k, vt, ot, gc, gr = _mlstm_proj_t(xp, g_mix, w, tm)
            ap, c_p, n_p, m_p = _mlstm_scan(qt, k, vt, ot, gc, gr, ghn, batch, chunk)
            mcp.append(c_p); mnp.append(n_p); mmp.append(m_p[:, :M_HEADS, 0])
            q, k, v, o, gc, _ = _mlstm_proj(xs, g_mix, w, dec)
            as_, c_new, n_s, m_s = _mlstm_step(q, k, v, o, gc, ghn, c_all, n_all, state_mlstm_m[j].astype(F32),
                                               j, min(DEC_SAMPLES, dec), c_new)
            mns.append(n_s); mms.append(m_s)
            w_out = mlstm_w_out[j].astype(BF16)
        else:
            w = _fox_weights(fox_w_in[j], fox_b_f[j])
            qt, ka, k32t, vta, v32t, lft, crow = _fox_proj(xp, g_mix, w, batch, tm, j, depth // 2, kv32)
            kv32 = (k32t, v32t)
            flp.append(lft)
            qd, kd, vd, lfd, qrd = _fox_proj_sample(xs, g_mix, w)
            ap, as_ = _fox_attn_decode(qt, ka, vta, crow, tm, qd, kd, vd, lfd, qrd,
                                       cache_fox_k, cache_fox_v, cache_fox_logf, page_table, j)
            fks.append(kd.reshape(1, F_HEADS, hd, dec)); fvs.append(vd.reshape(1, F_HEADS, hd, dec))
            fls.append(lfd.reshape(1, F_HEADS, dec))
            w_out = fox_w_out[j].astype(BF16)
        g_ffn = norm_ffn_g[layer].astype(F32)[None, :]
        w_up = ffn_w_up[layer].astype(BF16)
        w_down = ffn_w_down[layer].astype(BF16)
        final = layer == depth - 1
        xp = _mix_ffn(xp, ap, w_out, g_ffn, w_up, w_down, g_final, tm, final, a_transposed=layer % 2 == 0)
        xs = _mix_ffn(xs, as_, w_out, g_ffn, w_up, w_down, g_final, dec, final)

    fkp, fvp = (a.reshape(a.shape[0], batch, F_HEADS, hd, seq) for a in kv32)
    return (xp.reshape(batch, seq, d), xs.reshape(dec, 1, d),
            jnp.stack(mcp), jnp.stack(mnp), jnp.stack(mmp),
            c_new, jnp.stack(mns), jnp.stack(mms),
            jnp.transpose(fkp, (0, 1, 4, 2, 3)), jnp.transpose(fvp, (0, 1, 4, 2, 3)),
            jnp.transpose(jnp.stack(flp), (0, 1, 3, 2)),
            jnp.transpose(jnp.stack(fks), (0, 4, 1, 2, 3)), jnp.transpose(jnp.stack(fvs), (0, 4, 1, 2, 3)),
            jnp.transpose(jnp.stack(fls), (0, 3, 1, 2)))
```
